```python
import math
import jax, jax.numpy as jnp
from jax import lax
import numpy as np

D_MODEL = 1024
BATCH = 16
SEQ = 2048
DEPTH = 2

D_MIX = D_MODEL
HG_WIDTH = D_MIX // 4
HG_HEADS = 4
HG_DK = HG_WIDTH // HG_HEADS
HG_CHUNK = 32
S5_WIDTH = D_MIX // 4
S5_GROUP = 16
S5_GROUPS = S5_WIDTH // S5_GROUP
S5_STATE = 64
ML_WIDTH = D_MIX - HG_WIDTH - S5_WIDTH
ML_HEADS = 4
ML_DH = ML_WIDTH // ML_HEADS
ML_CHUNK = 64
CONV_WIDTH = 4
D_FF = -(-8 * D_MODEL // (3 * 256)) * 256
D_IN = 4 * HG_WIDTH + S5_WIDTH + 2 * ML_WIDTH
EPS = 1e-6

kernel_name = "hybrid_hgrn2_s5_mlstm_block"


def rmsnorm(x, g):
    xf = x.astype(jnp.float32)
    y = xf * lax.rsqrt(jnp.mean(xf * xf, axis=-1, keepdims=True) + EPS)
    return (y * g.astype(jnp.float32)).astype(x.dtype)


def causal_conv(x, w, b):
    S = x.shape[1]
    K = w.shape[0]
    xp = jnp.pad(x, ((0, 0), (K - 1, 0), (0, 0)))
    y = xp[:, 0:S] * w[0]
    for k in range(1, K):
        y = y + xp[:, k:k + S] * w[k]
    return y + b


def hgrn2_mix(q_raw, f_raw, i_raw, g_raw, lb, norm_g):
    B, S, _ = q_raw.shape
    H, dk, L = HG_HEADS, HG_DK, HG_CHUNK
    N = S // L
    q = jax.nn.silu(q_raw)
    zf = f_raw.astype(jnp.float32)
    lb = lb.astype(jnp.float32)
    log_f = jnp.logaddexp(jnp.log(lb), jnp.log1p(-lb) + jax.nn.log_sigmoid(zf))
    k = (1.0 - lb) * jax.nn.sigmoid(-zf)

    def chunk(t):
        return t.reshape(B, N, L, H, -1).transpose(0, 3, 1, 2, 4)

    qc, kc, vc, gc = chunk(q), chunk(k), chunk(i_raw), chunk(log_f)
    b = jnp.cumsum(gc, axis=3)
    b_last = b[:, :, :, -1]
    q_dec = qc * jnp.exp(b)
    k_inv = kc * jnp.exp(-b)
    k_end = kc * jnp.exp(b_last[:, :, :, None] - b)
    causal = jnp.tril(jnp.ones((L, L), dtype=bool))
    scores = jnp.where(causal, jnp.einsum('bhntd,bhnsd->bhnts', q_dec, k_inv), 0.0)
    o_intra = jnp.einsum('bhnts,bhnse->bhnte', scores, vc)
    dS = jnp.einsum('bhnsd,bhnse->bhnde', k_end, vc)
    decay = jnp.exp(b_last)

    def step(S_, inp):
        dec, ds = inp
        return dec[..., None] * S_ + ds, S_

    S0 = jnp.zeros((B, H, dk, vc.shape[-1]), dS.dtype)
    _, S_prev = lax.scan(step, S0, (jnp.moveaxis(decay, 2, 0), jnp.moveaxis(dS, 2, 0)))
    S_prev = jnp.moveaxis(S_prev, 0, 2)
    o_inter = jnp.einsum('bhntd,bhnde->bhnte', q_dec, S_prev)
    o = (o_intra + o_inter).transpose(0, 2, 3, 1, 4).reshape(B, S, H, -1)
    o = rmsnorm(o, norm_g.reshape(H, -1))
    o = o * jax.nn.silu(g_raw.reshape(B, S, H, -1))
    return o.reshape(B, S, HG_WIDTH)


def complex_affine_combine(e1, e2):
    a1r, a1i, b1r, b1i = e1
    a2r, a2i, b2r, b2i = e2
    return (a2r * a1r - a2i * a1i,
            a2r * a1i + a2i * a1r,
            a2r * b1r - a2i * b1i + b2r,
            a2r * b1i + a2i * b1r + b2i)


def s5_mix(u, lam_re, lam_im, log_dt, b_re, b_im, c_re, c_im, d, w_glu):
    B, S, _ = u.shape
    G, P = S5_GROUPS, S5_STATE
    ug = u.reshape(B, S, G, S5_GROUP)
    lre = lam_re.astype(jnp.float32)
    lim = lam_im.astype(jnp.float32)
    dt = jnp.exp(log_dt.astype(jnp.float32))[:, None]
    mag = jnp.exp(lre * dt)
    ang = lim * dt
    ar, ai = mag * jnp.cos(ang), mag * jnp.sin(ang)
    den = lre * lre + lim * lim
    nr, ni = ar - 1.0, ai
    cr = (nr * lre + ni * lim) / den
    ci = (ni * lre - nr * lim) / den
    bbr = cr[..., None] * b_re - ci[..., None] * b_im
    bbi = cr[..., None] * b_im + ci[..., None] * b_re
    bu_r = jnp.einsum('bsgh,gph->bsgp', ug, bbr)
    bu_i = jnp.einsum('bsgh,gph->bsgp', ug, bbi)
    a_r = jnp.broadcast_to(ar[None, None], (1, S, G, P))
    a_i = jnp.broadcast_to(ai[None, None], (1, S, G, P))
    _, _, xr, xi = lax.associative_scan(complex_affine_combine, (a_r, a_i, bu_r, bu_i), axis=1)
    y = jnp.einsum('bsgp,ghp->bsgh', xr, c_re) - jnp.einsum('bsgp,ghp->bsgh', xi, c_im)
    y = y.reshape(B, S, S5_WIDTH) + d * u
    y = jax.nn.gelu(y)
    return y * jax.nn.sigmoid(y @ w_glu)


def mlstm_mix(x_m, z, conv_w, conv_b, wq, wk, wv, w_gates, b_gates, norm_g, skip):
    B, S, _ = x_m.shape
    H, dh, L = ML_HEADS, ML_DH, ML_CHUNK
    N = S // L
    xc = jax.nn.silu(causal_conv(x_m, conv_w, conv_b))
    xch = xc.reshape(B, S, H, dh)
    xmh = x_m.reshape(B, S, H, dh)
    q = jnp.einsum('bshd,hde->bshe', xch, wq)
    k = jnp.einsum('bshd,hde->bshe', xch, wk) * (dh ** -0.5)
    v = jnp.einsum('bshd,hde->bshe', xmh, wv)
    qkv = jnp.concatenate([q.reshape(B, S, -1), k.reshape(B, S, -1), v.reshape(B, S, -1)], axis=-1)
    gates = (qkv @ w_gates + b_gates).astype(jnp.float32)
    ig = gates[..., :H]
    log_f = jax.nn.log_sigmoid(gates[..., H:])

    def chunk(t):
        return t.reshape(B, N, L, H, -1).transpose(0, 3, 1, 2, 4)

    def chunk_g(t):
        return t.reshape(B, N, L, H).transpose(0, 3, 1, 2)

    qc, kc, vc = chunk(q), chunk(k), chunk(v)
    igc, lfc = chunk_g(ig), chunk_g(log_f)
    b = jnp.cumsum(lfc, axis=-1)
    b_last = b[..., -1]
    a = b_last[..., None] - b + igc
    m_loc = jnp.max(a, axis=-1)
    w_loc = jnp.exp(a - m_loc[..., None])
    dC = jnp.einsum('bhns,bhnsd,bhnse->bhnde', w_loc, kc, vc)
    dn = jnp.einsum('bhns,bhnsd->bhnd', w_loc, kc)

    def step(carry, inp):
        C_, n_, m_ = carry
        g, ml, dc, dnn = inp
        m_new = jnp.maximum(g + m_, ml)
        sp = jnp.exp(g + m_ - m_new)
        sl = jnp.exp(ml - m_new)
        C_new = sp[..., None, None] * C_ + sl[..., None, None] * dc
        n_new = sp[..., None] * n_ + sl[..., None] * dnn
        return (C_new, n_new, m_new), (C_, n_, m_)

    init = (jnp.zeros((B, H, dh, dh), dC.dtype), jnp.zeros((B, H, dh), dn.dtype),
            jnp.zeros((B, H), m_loc.dtype))
    xs = (jnp.moveaxis(b_last, 2, 0), jnp.moveaxis(m_loc, 2, 0),
          jnp.moveaxis(dC, 2, 0), jnp.moveaxis(dn, 2, 0))
    _, (C_prev, n_prev, m_prev) = lax.scan(step, init, xs)
    C_prev = jnp.moveaxis(C_prev, 0, 2)
    n_prev = jnp.moveaxis(n_prev, 0, 2)
    m_prev = jnp.moveaxis(m_prev, 0, 2)

    causal = jnp.tril(jnp.ones((L, L), dtype=bool))
    logD = jnp.where(causal, b[..., :, None] - b[..., None, :] + igc[..., None, :], -jnp.inf)
    inter = b + m_prev[..., None]
    m_t = jnp.maximum(jnp.max(logD, axis=-1), inter)
    Dm = jnp.exp(logD - m_t[..., None])
    W = jnp.einsum('bhntd,bhnsd->bhnts', qc, kc) * Dm
    sc = jnp.exp(inter - m_t)
    num = (jnp.einsum('bhnts,bhnse->bhnte', W, vc)
           + sc[..., None] * jnp.einsum('bhntd,bhnde->bhnte', qc, C_prev))
    den = jnp.sum(W, axis=-1) + sc * jnp.einsum('bhntd,bhnd->bhnt', qc, n_prev)
    h = num / jnp.maximum(jnp.abs(den), jnp.exp(-m_t))[..., None]
    h = h.transpose(0, 2, 3, 1, 4).reshape(B, S, H, dh)
    h = rmsnorm(h, norm_g.reshape(H, dh)).reshape(B, S, ML_WIDTH)
    return (h + skip * xc) * jax.nn.silu(z)


def setup_inputs(seed: int = 0) -> dict:
    key = jax.random.key(seed)
    ks = iter(jax.random.split(key, 40))

    def nrm(shape, scale):
        return jax.random.normal(next(ks), shape, jnp.float32) * scale

    L = DEPTH
    G, P, H = S5_GROUPS, S5_STATE, ML_HEADS
    n_idx = jnp.arange(P, dtype=jnp.float32)
    log_dt_min, log_dt_max = math.log(0.001), math.log(0.1)
    return {
        "x": nrm((BATCH, SEQ, D_MODEL), 1.0),
        "norm_mix_g": 1.0 + nrm((L, D_MODEL), 0.02),
        "w_in": nrm((L, D_MODEL, D_IN), D_MODEL ** -0.5),
        "hgrn_lb_raw": nrm((L, HG_WIDTH), 0.5),
        "hgrn_norm_g": 1.0 + nrm((L, HG_WIDTH), 0.02),
        "s5_lambda_re": -0.5 + nrm((L, G, P), 0.01),
        "s5_lambda_im": math.pi * n_idx + nrm((L, G, P), 0.01),
        "s5_log_dt": log_dt_min + jax.random.uniform(next(ks), (L, G), jnp.float32) * (log_dt_max - log_dt_min),
        "s5_b_re": nrm((L, G, P, S5_GROUP), (2 * S5_GROUP) ** -0.5),
        "s5_b_im": nrm((L, G, P, S5_GROUP), (2 * S5_GROUP) ** -0.5),
        "s5_c_re": nrm((L, G, S5_GROUP, P), P ** -0.5),
        "s5_c_im": nrm((L, G, S5_GROUP, P), P ** -0.5),
        "s5_d": nrm((L, S5_WIDTH), 1.0),
        "s5_w_glu": nrm((L, S5_WIDTH, S5_WIDTH), S5_WIDTH ** -0.5),
        "ml_conv_w": nrm((L, CONV_WIDTH, ML_WIDTH), CONV_WIDTH ** -0.5),
        "ml_conv_b": nrm((L, ML_WIDTH), 0.02),
        "ml_wq": nrm((L, H, ML_DH, ML_DH), ML_DH ** -0.5),
        "ml_wk": nrm((L, H, ML_DH, ML_DH), ML_DH ** -0.5),
        "ml_wv": nrm((L, H, ML_DH, ML_DH), ML_DH ** -0.5),
        "ml_w_gates": nrm((L, 3 * ML_WIDTH, 2 * H), 0.1 * (3 * ML_WIDTH) ** -0.5),
        "ml_b_gates": jnp.concatenate([nrm((L, H), 0.1),
                                       jnp.linspace(3.0, 6.0, H, dtype=jnp.float32)[None] + nrm((L, H), 0.01)], axis=-1),
        "ml_norm_g": 1.0 + nrm((L, ML_WIDTH), 0.02),
        "ml_skip": 1.0 + nrm((L, ML_WIDTH), 0.02),
        "w_out": nrm((L, D_MIX, D_MODEL), D_MIX ** -0.5),
        "norm_ffn_g": 1.0 + nrm((L, D_MODEL), 0.02),
        "ffn_w_gate": nrm((L, D_MODEL, D_FF), D_MODEL ** -0.5),
        "ffn_w_up": nrm((L, D_MODEL, D_FF), D_MODEL ** -0.5),
        "ffn_w_down": nrm((L, D_FF, D_MODEL), D_FF ** -0.5),
        "norm_final_g": 1.0 + nrm((D_MODEL,), 0.02),
    }


def reference(x, norm_mix_g, w_in, hgrn_lb_raw, hgrn_norm_g, s5_lambda_re, s5_lambda_im, s5_log_dt,
              s5_b_re, s5_b_im, s5_c_re, s5_c_im, s5_d, s5_w_glu, ml_conv_w, ml_conv_b, ml_wq, ml_wk,
              ml_wv, ml_w_gates, ml_b_gates, ml_norm_g, ml_skip, w_out, norm_ffn_g, ffn_w_gate,
              ffn_w_up, ffn_w_down, norm_final_g):
    lb_cs = jnp.cumsum(jax.nn.softmax(hgrn_lb_raw.astype(jnp.float32), axis=0), axis=0)
    lb_all = lb_cs - lb_cs[0]
    splits = [HG_WIDTH, 2 * HG_WIDTH, 3 * HG_WIDTH, 4 * HG_WIDTH,
              4 * HG_WIDTH + S5_WIDTH, 4 * HG_WIDTH + S5_WIDTH + ML_WIDTH]
    for l in range(DEPTH):
        h = rmsnorm(x, norm_mix_g[l])
        p = h @ w_in[l]
        hg_q, hg_f, hg_i, hg_g, s5_u, ml_x, ml_z = jnp.split(p, splits, axis=-1)
        o_hg = hgrn2_mix(hg_q, hg_f, hg_i, hg_g, lb_all[l], hgrn_norm_g[l])
        o_s5 = s5_mix(s5_u, s5_lambda_re[l], s5_lambda_im[l], s5_log_dt[l], s5_b_re[l], s5_b_im[l],
                      s5_c_re[l], s5_c_im[l], s5_d[l], s5_w_glu[l])
        o_ml = mlstm_mix(ml_x, ml_z, ml_conv_w[l], ml_conv_b[l], ml_wq[l], ml_wk[l], ml_wv[l],
                         ml_w_gates[l], ml_b_gates[l], ml_norm_g[l], ml_skip[l])
        mix = jnp.concatenate([o_hg, o_s5, o_ml], axis=-1) @ w_out[l]
        x = x + mix.astype(x.dtype)
        h = rmsnorm(x, norm_ffn_g[l])
        ff = (jax.nn.silu(h @ ffn_w_gate[l]) * (h @ ffn_w_up[l])) @ ffn_w_down[l]
        x = x + ff.astype(x.dtype)
    return rmsnorm(x, norm_final_g)
```

```python
import functools
import math

import numpy as np
import jax
import jax.numpy as jnp
from jax import lax
from jax.experimental import pallas as pl
from jax.experimental.pallas import tpu as pltpu

F32 = jnp.float32
BF16 = jnp.bfloat16

D_MODEL = 1024
HG_W = 256
HG_H = 4
HG_DK = HG_W // HG_H
HG_L = 32
S5_W = 256
S5_G = 16
S5_GC = 16
S5_P = 64
S5_N = S5_G * S5_P
ML_W = 512
ML_H = 4
ML_DH = ML_W // ML_H
CONV_K = 4
D_FF = 2816
D_IN = 4 * HG_W + S5_W + 2 * ML_W
EPS = 1e-6

NB = 8
TT = 64
R = NB * TT
FFN_TM = 512
FFN_TF = 256
VMEM_LIMIT = 58 * 1024 * 1024

_O_HQ, _O_HF, _O_HI, _O_HG, _O_SU, _O_MX, _O_MZ = 0, 256, 512, 768, 1024, 1280, 1792


def _dot(a, b):
    return jnp.dot(a, b, preferred_element_type=F32)


def _dot_nt(a, b):
    return lax.dot_general(a, b, (((1,), (1,)), ((), ())), preferred_element_type=F32)


def _dot_tn(a, b):
    return lax.dot_general(a, b, (((0,), (0,)), ((), ())), preferred_element_type=F32)


def _split3(x):
    hi = x.astype(BF16)
    r1 = x - hi.astype(F32)
    mid = r1.astype(BF16)
    lo = (r1 - mid.astype(F32)).astype(BF16)
    return hi, mid, lo


def _sel_dot(m01, x):
    hi, mid, lo = _split3(x)
    return _dot(m01, hi) + _dot(m01, mid) + _dot(m01, lo)


def _sigmoid(v):
    return 1.0 / (1.0 + jnp.exp(-v))


def _log_sigmoid(v):
    return jnp.minimum(v, 0.0) - jnp.log1p(jnp.exp(-jnp.abs(v)))


def _rms_scale(v):
    return lax.rsqrt(jnp.mean(v * v, axis=-1, keepdims=True) + EPS)


def _mix_kernel(x_ref, gmix_ref, win_ref,
                hgc_ref, hgng_ref, lc_ref, hgmask_ref, headblk_ref,
                perm_ref, permt_ref, bblk_ref, cblk_ref, s5a_ref, s5d_ref, wglu_ref,
                convw_ref, convb_ref, wq_ref, wk_ref, wv_ref, wgt_ref, bgt_ref, l64_ref, mlmask_ref,
                mlng_ref, mlskip_ref, wout_ref,
                o_ref,
                hst_ref, s5st_ref, bu_ref, c_ref, n_ref, m_ref, prev_ref,
                qd_ref, ke_ref, hv_ref, dec_ref, hoi_ref,
                mq_ref, mk_ref, mv_ref, mcomb_ref, mh_ref):
    si = pl.program_id(1)

    @pl.when(si == 0)
    def _reset_state():
        hst_ref[...] = jnp.zeros_like(hst_ref)
        s5st_ref[...] = jnp.zeros_like(s5st_ref)
        c_ref[...] = jnp.zeros_like(c_ref)
        n_ref[...] = jnp.zeros_like(n_ref)
        m_ref[...] = jnp.zeros_like(m_ref)
        prev_ref[...] = jnp.zeros_like(prev_ref)

    x = x_ref[...].reshape(R, D_MODEL)
    h = (x * _rms_scale(x) * gmix_ref[...]).astype(BF16)

    zq = _dot(h, win_ref[:, _O_HQ:_O_HQ + HG_W])
    zf = _dot(h, win_ref[:, _O_HF:_O_HF + HG_W])
    vi = _dot(h, win_ref[:, _O_HI:_O_HI + HG_W])
    zg = _dot(h, win_ref[:, _O_HG:_O_HG + HG_W])
    q = zq * _sigmoid(zq)
    log_lb = hgc_ref[0:1, :]
    log_1mlb = hgc_ref[1:2, :]
    one_mlb = hgc_ref[2:3, :]
    e = jnp.exp(-jnp.abs(zf))
    rcp = 1.0 / (1.0 + e)
    sig_neg = jnp.where(zf >= 0.0, e * rcp, rcp)
    a2 = log_1mlb + (jnp.minimum(zf, 0.0) - jnp.log1p(e))
    a1 = jnp.broadcast_to(log_lb, a2.shape)
    logf = jnp.maximum(a1, a2) + jnp.log1p(jnp.exp(-jnp.abs(a1 - a2)))
    kk = one_mlb * sig_neg

    lc = lc_ref[...]
    half = R // 2
    b = jnp.concatenate([_sel_dot(lc, logf[:half]), _sel_dot(lc, logf[half:])], axis=0)
    b3 = b.reshape(R // HG_L, HG_L, HG_W)
    blast = jnp.broadcast_to(b3[:, HG_L - 1:HG_L, :], b3.shape).reshape(R, HG_W)
    qd = q * jnp.exp(b)
    ki = kk * jnp.exp(-b)
    qd_ref[...] = qd.astype(BF16)
    ke_ref[...] = (kk * jnp.exp(blast - b)).astype(BF16)
    hv_ref[...] = vi.astype(BF16)
    dec_ref[...] = jnp.exp(blast)

    lane = lax.broadcasted_iota(jnp.int32, (1, HG_W), 1)
    head_lane = [(lane // HG_DK) == hh for hh in range(HG_H)]
    hgmask = hgmask_ref[...] > 0.5
    GR = 128
    oi_parts = []
    for g in range(R // GR):
        rs = slice(g * GR, (g + 1) * GR)
        ki_g = ki[rs]
        vi_g = vi[rs]
        kblk = jnp.concatenate([jnp.where(head_lane[hh], ki_g, 0.0) for hh in range(HG_H)], axis=0)
        sc = _dot_nt(qd[rs].astype(BF16), kblk.astype(BF16))
        sc = jnp.where(hgmask, sc, 0.0)
        vstk = jnp.concatenate([jnp.where(head_lane[hh], vi_g, 0.0) for hh in range(HG_H)], axis=0)
        oi_parts.append(_dot(sc.astype(BF16), vstk.astype(BF16)))
    o_intra = jnp.concatenate(oi_parts, axis=0)

    headblk_bf = headblk_ref[...]
    headblk = headblk_bf.astype(F32) > 0.5

    def hg_batch(bi, carry):
        st = hst_ref[bi]
        for cc in range(TT // HG_L):
            r0 = pl.multiple_of(bi * TT + cc * HG_L, HG_L)
            rows = pl.ds(r0, HG_L)
            hoi_ref[rows, :] = _dot_nt(qd_ref[rows, :], st.astype(BF16))
            dst = _dot_tn(hv_ref[rows, :], ke_ref[rows, :])
            st = dec_ref[pl.ds(r0, 1), :] * st + jnp.where(headblk, dst, 0.0)
        hst_ref[bi] = st
        return carry

    lax.fori_loop(0, NB, hg_batch, 0)

    o = o_intra + hoi_ref[...]
    o2 = o * o
    o2_hi = o2.astype(BF16)
    o2_lo = (o2 - o2_hi.astype(F32)).astype(BF16)
    ms_h = (_dot(o2_hi, headblk_bf) + _dot(o2_lo, headblk_bf)) * (1.0 / HG_DK)
    o_hg = o * lax.rsqrt(ms_h + EPS) * hgng_ref[...] * (zg * _sigmoid(zg))

    u = _dot(h, win_ref[:, _O_SU:_O_SU + S5_W])
    u_tb = _dot(perm_ref[...], u.astype(BF16)).astype(BF16)
    bu_ref[...] = _dot(u_tb, bblk_ref[...]).reshape(TT, NB, 2 * S5_N)
    ar = s5a_ref[0:1, :]
    ai = s5a_ref[1:2, :]

    def s5_step(t, carry):
        xr, xi = carry
        but = bu_ref[t]
        nr = ar * xr - ai * xi + but[:, :S5_N]
        ni = ar * xi + ai * xr + but[:, S5_N:]
        bu_ref[t] = jnp.concatenate([nr, ni], axis=-1)
        return nr, ni

    x0 = s5st_ref[...]
    xr, xi = lax.fori_loop(0, TT, s5_step, (x0[:, :S5_N], x0[:, S5_N:]), unroll=8)
    s5st_ref[...] = jnp.concatenate([xr, xi], axis=-1)
    xs = bu_ref[...].reshape(R, 2 * S5_N)
    y_tb = _dot(xs.astype(BF16), cblk_ref[...])
    y = _sel_dot(permt_ref[...], y_tb) + s5d_ref[...] * u
    y = 0.5 * y * (1.0 + jnp.tanh(math.sqrt(2.0 / math.pi) * (y + 0.044715 * (y * y * y))))
    o_s5 = y * _sigmoid(_dot(y.astype(BF16), wglu_ref[...]))

    xm = _dot(h, win_ref[:, _O_MX:_O_MX + ML_W])
    z = _dot(h, win_ref[:, _O_MZ:_O_MZ + ML_W])
    prev = prev_ref[...]
    tcol = lax.broadcasted_iota(jnp.int32, (R, 1), 0) % TT
    conv = xm * convw_ref[CONV_K - 1:CONV_K, :]
    for k in range(1, CONV_K):
        shifted = jnp.where(tcol >= k, pltpu.roll(xm, k, 0), pltpu.roll(prev, k + R - TT, 0))
        conv = conv + shifted * convw_ref[CONV_K - 1 - k:CONV_K - k, :]
    conv = conv + convb_ref[...]
    prev_ref[...] = xm
    xc = conv * _sigmoid(conv)
    xcb = xc.astype(BF16)
    xmb = xm.astype(BF16)
    qs, ks, vs = [], [], []
    for hh in range(ML_H):
        ls = slice(hh * ML_DH, (hh + 1) * ML_DH)
        q_h = _dot(xcb[:, ls], wq_ref[hh])
        k_h = _dot(xcb[:, ls], wk_ref[hh]) * (ML_DH ** -0.5)
        v_h = _dot(xmb[:, ls], wv_ref[hh])
        mq_ref[hh] = q_h.astype(BF16)
        mk_ref[hh] = k_h
        mv_ref[hh] = v_h.astype(BF16)
        qs.append(q_h)
        ks.append(k_h)
        vs.append(v_h)
    qkv = jnp.concatenate(qs + ks + vs, axis=-1).astype(BF16)
    gates = _dot(qkv, wgt_ref[...]) + bgt_ref[...]
    bcum = _sel_dot(l64_ref[...], _log_sigmoid(gates))
    glane = lax.broadcasted_iota(jnp.int32, (1, 128), 1)
    mcomb_ref[...] = jnp.where(glane < ML_H, gates, bcum)

    mlmask = mlmask_ref[...] > 0.5
    PR = 2 * TT

    def ml_pair(pj, carry):
        r0 = pl.multiple_of(pj * PR, PR)
        rows = pl.ds(r0, PR)
        comb = mcomb_ref[rows, :]
        comb_t = comb.T
        for hh in range(ML_H):
            ig_c = comb[:, hh:hh + 1]
            b_c = comb[:, ML_H + hh:ML_H + hh + 1]
            ig_r = comb_t[hh:hh + 1, :]
            b_r = comb_t[ML_H + hh:ML_H + hh + 1, :]
            idx = [(pj * 2 + j) * ML_H + hh for j in range(2)]
            m_prev = [m_ref[idx[j]][:, 0:1] for j in range(2)]
            m_prev_col = jnp.concatenate([jnp.broadcast_to(m_prev[j], (TT, 1)) for j in range(2)], axis=0)
            q_p = mq_ref[hh, rows, :]
            k_p = mk_ref[hh, rows, :]
            v_p = mv_ref[hh, rows, :]
            log_d = jnp.where(mlmask, b_c - b_r + ig_r, -jnp.inf)
            inter = b_c + m_prev_col
            m_t = jnp.maximum(jnp.max(log_d, axis=-1, keepdims=True), inter)
            w = _dot_nt(q_p, k_p.astype(BF16)) * jnp.exp(log_d - m_t)
            scl = jnp.exp(inter - m_t)
            qc_parts, qn_parts = [], []
            for j in range(2):
                js = slice(j * TT, (j + 1) * TT)
                c_prev = c_ref[idx[j]]
                n_prev = n_ref[idx[j]]
                q_b = q_p[js]
                qc_parts.append(_dot(q_b, c_prev.astype(BF16)))
                qn_parts.append(jnp.sum(q_b.astype(F32) * n_prev, axis=-1, keepdims=True))
                b_b = b_c[js]
                g_last = b_b[TT - 1:TT, :]
                a_b = g_last - b_b + ig_c[js]
                m_loc = jnp.max(a_b, axis=0, keepdims=True)
                wk = jnp.exp(a_b - m_loc) * k_p[js]
                d_c = _dot_tn(wk.astype(BF16), v_p[js])
                d_n = jnp.sum(wk, axis=0, keepdims=True)
                m_new = jnp.maximum(g_last + m_prev[j], m_loc)
                s_prev = jnp.exp(g_last + m_prev[j] - m_new)
                s_loc = jnp.exp(m_loc - m_new)
                c_ref[idx[j]] = s_prev * c_prev + s_loc * d_c
                n_ref[idx[j]] = s_prev * n_prev + s_loc * d_n
                m_ref[idx[j]] = jnp.broadcast_to(m_new, (1, 128))
            num = _dot(w.astype(BF16), v_p) + scl * jnp.concatenate(qc_parts, axis=0)
            den = jnp.sum(w, axis=-1, keepdims=True) + scl * jnp.concatenate(qn_parts, axis=0)
            hout = num / jnp.maximum(jnp.abs(den), jnp.exp(-m_t))
            mh_ref[rows, hh * ML_DH:(hh + 1) * ML_DH] = hout * _rms_scale(hout)
        return carry

    lax.fori_loop(0, NB // 2, ml_pair, 0)
    o_ml = (mh_ref[...] * mlng_ref[...] + mlskip_ref[...] * xc) * (z * _sigmoid(z))

    mixcat = jnp.concatenate([o_hg, o_s5, o_ml], axis=-1).astype(BF16)
    out = x + _dot(mixcat, wout_ref[...])
    o_ref[...] = out.reshape(NB, TT, D_MODEL)


def _ffn_kernel(x_ref, g_ref, wg_ref, wu_ref, wd_ref, gfin_ref, o_ref, *, final_norm):
    x = x_ref[...]
    h = (x * _rms_scale(x) * g_ref[...]).astype(BF16)
    acc = x
    for c in range(D_FF // FFN_TF):
        cs = slice(c * FFN_TF, (c + 1) * FFN_TF)
        gate = _dot(h, wg_ref[:, cs])
        up = _dot(h, wu_ref[:, cs])
        act = (gate * _sigmoid(gate) * up).astype(BF16)
        acc = acc + _dot(act, wd_ref[cs, :])
    if final_norm:
        acc = acc * _rms_scale(acc) * gfin_ref[...]
    o_ref[...] = acc


def _np_masks():
    r = np.arange(R)
    perm = np.zeros((R, R), np.float32)
    perm[(r % TT) * NB + (r // TT), r] = 1.0
    i256 = np.arange(256)
    lc = ((i256[:, None] // HG_L) == (i256[None, :] // HG_L)) & (i256[None, :] <= i256[:, None])
    i128 = np.arange(128)
    hg_causal = ((i128[:, None] // HG_L) == (i128[None, :] // HG_L)) & (i128[None, :] <= i128[:, None])
    hgmask = np.tile(hg_causal, (1, HG_H))
    headblk = (i256[:, None] // HG_DK) == (i256[None, :] // HG_DK)
    l64 = ((r[:, None] // TT) == (r[None, :] // TT)) & (r[None, :] <= r[:, None])
    mlmask = ((i128[:, None] // TT) == (i128[None, :] // TT)) & (i128[None, :] <= i128[:, None])
    return dict(perm=perm, permt=perm.T.copy(), lc=lc.astype(np.float32), hgmask=hgmask.astype(np.float32),
                headblk=headblk.astype(np.float32), l64=l64.astype(np.float32),
                mlmask=mlmask.astype(np.float32))


_MASKS = _np_masks()


def _const_spec(shape):
    nd = len(shape)
    return pl.BlockSpec(shape, lambda *_: (0,) * nd, pipeline_mode=pl.Buffered(1))


def _pad_rows(a, rows):
    return jnp.concatenate([a, jnp.zeros((rows - a.shape[0],) + a.shape[1:], a.dtype)], axis=0)


def _s5_params(lam_re, lam_im, log_dt, b_re, b_im, c_re, c_im):
    lre = lam_re.astype(F32)
    lim = lam_im.astype(F32)
    dt = jnp.exp(log_dt.astype(F32))[:, None]
    mag = jnp.exp(lre * dt)
    ang = lim * dt
    ar, ai = mag * jnp.cos(ang), mag * jnp.sin(ang)
    den = lre * lre + lim * lim
    nr, ni = ar - 1.0, ai
    cr = (nr * lre + ni * lim) / den
    ci = (ni * lre - nr * lim) / den
    bbr = cr[..., None] * b_re - ci[..., None] * b_im
    bbi = cr[..., None] * b_im + ci[..., None] * b_re
    eye = jnp.eye(S5_G, dtype=F32)
    blk_r = jnp.einsum('gpc,gk->gckp', bbr, eye).reshape(S5_W, S5_N)
    blk_i = jnp.einsum('gpc,gk->gckp', bbi, eye).reshape(S5_W, S5_N)
    bblk = jnp.concatenate([blk_r, blk_i], axis=1).astype(BF16)
    cb_r = jnp.einsum('gcp,gk->kpgc', c_re, eye).reshape(S5_N, S5_W)
    cb_i = jnp.einsum('gcp,gk->kpgc', c_im, eye).reshape(S5_N, S5_W)
    cblk = jnp.concatenate([cb_r, -cb_i], axis=0).astype(BF16)
    a = _pad_rows(jnp.stack([ar.reshape(S5_N), ai.reshape(S5_N)], axis=0), 8)
    return bblk, cblk, a


def _mix_layer(x, gmix, w_in, lb, hg_ng, s5, s5_d, w_glu, conv_w, conv_b, wq, wk, wv, w_gates, b_gates,
               ml_ng, ml_skip, w_out):
    B, S, _ = x.shape
    assert B % NB == 0 and S % TT == 0
    mk = _MASKS
    hgc = _pad_rows(jnp.stack([jnp.log(lb), jnp.log1p(-lb), 1.0 - lb], axis=0), 8)
    bblk, cblk, s5a = s5
    wgt = jnp.concatenate([w_gates, jnp.zeros((3 * ML_W, 128 - 2 * ML_H), F32)], axis=1).astype(BF16)
    bgt = jnp.concatenate([b_gates, jnp.zeros((128 - 2 * ML_H,), F32)])[None, :]
    consts = [
        gmix[None, :], w_in.astype(BF16),
        hgc, hg_ng[None, :], jnp.asarray(mk['lc'], BF16), jnp.asarray(mk['hgmask'], F32),
        jnp.asarray(mk['headblk'], BF16),
        jnp.asarray(mk['perm'], BF16), jnp.asarray(mk['permt'], BF16), bblk, cblk, s5a, s5_d[None, :],
        w_glu.astype(BF16),
        _pad_rows(conv_w, 8), conv_b[None, :], wq.astype(BF16), wk.astype(BF16), wv.astype(BF16), wgt, bgt,
        jnp.asarray(mk['l64'], BF16), jnp.asarray(mk['mlmask'], F32),
        ml_ng[None, :], ml_skip[None, :], w_out.astype(BF16),
    ]
    xspec = pl.BlockSpec((NB, TT, D_MODEL), lambda bi, si: (bi, si, 0))
    scratch = [
        pltpu.VMEM((NB, HG_W, HG_W), F32),
        pltpu.VMEM((NB, 2 * S5_N), F32),
        pltpu.VMEM((TT, NB, 2 * S5_N), F32),
        pltpu.VMEM((NB * ML_H, ML_DH, ML_DH), F32),
        pltpu.VMEM((NB * ML_H, 1, ML_DH), F32),
        pltpu.VMEM((NB * ML_H, 1, 128), F32),
        pltpu.VMEM((R, ML_W), F32),
        pltpu.VMEM((R, HG_W), BF16),
        pltpu.VMEM((R, HG_W), BF16),
        pltpu.VMEM((R, HG_W), BF16),
        pltpu.VMEM((R, HG_W), F32),
        pltpu.VMEM((R, HG_W), F32),
        pltpu.VMEM((ML_H, R, ML_DH), BF16),
        pltpu.VMEM((ML_H, R, ML_DH), F32),
        pltpu.VMEM((ML_H, R, ML_DH), BF16),
        pltpu.VMEM((R, 128), F32),
        pltpu.VMEM((R, ML_W), F32),
    ]
    return pl.pallas_call(
        _mix_kernel,
        grid=(B // NB, S // TT),
        in_specs=[xspec] + [_const_spec(c.shape) for c in consts],
        out_specs=xspec,
        out_shape=jax.ShapeDtypeStruct(x.shape, F32),
        scratch_shapes=scratch,
        compiler_params=pltpu.CompilerParams(
            dimension_semantics=("parallel", "arbitrary"), vmem_limit_bytes=VMEM_LIMIT),
        name="mixer",
    )(x, *consts)


def _ffn_layer(x2, g, wg, wu, wd, gfin, final_norm):
    T = x2.shape[0]
    assert T % FFN_TM == 0
    consts = [g[None, :], wg.astype(BF16), wu.astype(BF16), wd.astype(BF16), gfin[None, :]]
    xspec = pl.BlockSpec((FFN_TM, D_MODEL), lambda i: (i, 0))
    return pl.pallas_call(
        functools.partial(_ffn_kernel, final_norm=final_norm),
        grid=(T // FFN_TM,),
        in_specs=[xspec] + [_const_spec(c.shape) for c in consts],
        out_specs=xspec,
        out_shape=jax.ShapeDtypeStruct(x2.shape, F32),
        compiler_params=pltpu.CompilerParams(
            dimension_semantics=("parallel",), vmem_limit_bytes=VMEM_LIMIT),
        name="ffn",
    )(x2, *consts)


def kernel(x, norm_mix_g, w_in, hgrn_lb_raw, hgrn_norm_g, s5_lambda_re, s5_lambda_im, s5_log_dt, s5_b_re, s5_b_im, s5_c_re, s5_c_im, s5_d, s5_w_glu, ml_conv_w, ml_conv_b, ml_wq, ml_wk, ml_wv, ml_w_gates, ml_b_gates, ml_norm_g, ml_skip, w_out, norm_ffn_g, ffn_w_gate, ffn_w_up, ffn_w_down, norm_final_g):
    B, S, D = x.shape
    depth = w_in.shape[0]
    lb_cs = jnp.cumsum(jax.nn.softmax(hgrn_lb_raw.astype(F32), axis=0), axis=0)
    lb_all = lb_cs - lb_cs[0]
    for l in range(depth):
        s5 = _s5_params(s5_lambda_re[l], s5_lambda_im[l], s5_log_dt[l], s5_b_re[l], s5_b_im[l],
                        s5_c_re[l], s5_c_im[l])
        x = _mix_layer(x, norm_mix_g[l], w_in[l], lb_all[l], hgrn_norm_g[l], s5, s5_d[l], s5_w_glu[l],
                       ml_conv_w[l], ml_conv_b[l], ml_wq[l], ml_wk[l], ml_wv[l], ml_w_gates[l],
                       ml_b_gates[l], ml_norm_g[l], ml_skip[l], w_out[l])
        x2 = _ffn_layer(x.reshape(B * S, D), norm_ffn_g[l], ffn_w_gate[l], ffn_w_up[l], ffn_w_down[l],
                        norm_final_g, final_norm=(l == depth - 1))
        x = x2.reshape(B, S, D)
    return x
```

```python
import functools
import math

import numpy as np
import jax
import jax.numpy as jnp
from jax import lax
from jax.experimental import pallas as pl
from jax.experimental.pallas import tpu as pltpu

F32 = jnp.float32
BF16 = jnp.bfloat16

D_MODEL = 1024
HG_W = 256
HG_H = 4
HG_DK = HG_W // HG_H
HG_L = 32
S5_W = 256
S5_G = 16
S5_GC = 16
S5_P = 64
S5_N = S5_G * S5_P
ML_W = 512
ML_H = 4
ML_DH = ML_W // ML_H
CONV_K = 4
D_FF = 2816
D_IN = 4 * HG_W + S5_W + 2 * ML_W
EPS = 1e-6

NB = 8
TT = 64
R = NB * TT
FFN_TM = 512
FFN_TF = 256
VMEM_LIMIT = 58 * 1024 * 1024

_O_HQ, _O_HF, _O_HI, _O_HG, _O_SU, _O_MX, _O_MZ = 0, 256, 512, 768, 1024, 1280, 1792


def _dot(a, b):
    return jnp.dot(a, b, preferred_element_type=F32)


def _dot_nt(a, b):
    return lax.dot_general(a, b, (((1,), (1,)), ((), ())), preferred_element_type=F32)


def _dot_tn(a, b):
    return lax.dot_general(a, b, (((0,), (0,)), ((), ())), preferred_element_type=F32)


def _split3(x):
    hi = x.astype(BF16)
    r1 = x - hi.astype(F32)
    mid = r1.astype(BF16)
    lo = (r1 - mid.astype(F32)).astype(BF16)
    return hi, mid, lo


def _sel_dot(m01, x):
    hi, mid, lo = _split3(x)
    return _dot(m01, hi) + _dot(m01, mid) + _dot(m01, lo)


def _sel_dot_r(x, m01):
    hi, mid, lo = _split3(x)
    return _dot(hi, m01) + _dot(mid, m01) + _dot(lo, m01)


def _sigmoid(v):
    return 1.0 / (1.0 + jnp.exp(-v))


def _log_sigmoid(v):
    return jnp.minimum(v, 0.0) - jnp.log1p(jnp.exp(-jnp.abs(v)))


def _rms_scale(v):
    return lax.rsqrt(jnp.mean(v * v, axis=-1, keepdims=True) + EPS)


def _mix_kernel(x_ref, gmix_ref, win_ref,
                hgc_ref, hgng_ref, lc_ref, hgmask_ref, headblk_ref,
                perm_ref, permt_ref, bblk_ref, cblk_ref, s5a_ref, s5d_ref, wglu_ref,
                convw_ref, convb_ref, wq_ref, wk_ref, wvt_ref, wgq_ref, wgk_ref, wgv_ref, bg_ref,
                l64t_ref, selall_ref, mlmask_t_ref, mlng_ref, mlskip_ref, wout_ref,
                o_ref,
                hst_ref, s5st_ref, bu_ref, ct_ref, n_ref, msb_ref, prev_ref,
                qd_ref, ke_ref, hv_ref, dec_ref, hoi_ref, mh_ref):
    si = pl.program_id(1)

    @pl.when(si == 0)
    def _reset_state():
        hst_ref[...] = jnp.zeros_like(hst_ref)
        s5st_ref[...] = jnp.zeros_like(s5st_ref)
        ct_ref[...] = jnp.zeros_like(ct_ref)
        n_ref[...] = jnp.zeros_like(n_ref)
        msb_ref[...] = jnp.zeros_like(msb_ref)
        prev_ref[...] = jnp.zeros_like(prev_ref)

    x = x_ref[...].reshape(R, D_MODEL)
    h = (x * _rms_scale(x) * gmix_ref[...]).astype(BF16)

    zq = _dot(h, win_ref[:, _O_HQ:_O_HQ + HG_W])
    zf = _dot(h, win_ref[:, _O_HF:_O_HF + HG_W])
    vi = _dot(h, win_ref[:, _O_HI:_O_HI + HG_W])
    zg = _dot(h, win_ref[:, _O_HG:_O_HG + HG_W])
    q = zq * _sigmoid(zq)
    log_lb = hgc_ref[0:1, :]
    log_1mlb = hgc_ref[1:2, :]
    one_mlb = hgc_ref[2:3, :]
    e = jnp.exp(-jnp.abs(zf))
    rcp = 1.0 / (1.0 + e)
    sig_neg = jnp.where(zf >= 0.0, e * rcp, rcp)
    a2 = log_1mlb + (jnp.minimum(zf, 0.0) - jnp.log1p(e))
    a1 = jnp.broadcast_to(log_lb, a2.shape)
    logf = jnp.maximum(a1, a2) + jnp.log1p(jnp.exp(-jnp.abs(a1 - a2)))
    kk = one_mlb * sig_neg

    lc = lc_ref[...]
    half = R // 2
    b = jnp.concatenate([_sel_dot(lc, logf[:half]), _sel_dot(lc, logf[half:])], axis=0)
    b3 = b.reshape(R // HG_L, HG_L, HG_W)
    blast = jnp.broadcast_to(b3[:, HG_L - 1:HG_L, :], b3.shape).reshape(R, HG_W)
    qd = q * jnp.exp(b)
    ki = kk * jnp.exp(-b)
    qd_ref[...] = qd.astype(BF16)
    ke_ref[...] = (kk * jnp.exp(blast - b)).astype(BF16)
    hv_ref[...] = vi.astype(BF16)
    dec_ref[...] = jnp.exp(blast)

    lane = lax.broadcasted_iota(jnp.int32, (1, HG_W), 1)
    head_lane = [(lane // HG_DK) == hh for hh in range(HG_H)]
    hgmask = hgmask_ref[...] > 0.5
    GR = 128
    oi_parts = []
    for g in range(R // GR):
        rs = slice(g * GR, (g + 1) * GR)
        ki_g = ki[rs]
        vi_g = vi[rs]
        kblk = jnp.concatenate([jnp.where(head_lane[hh], ki_g, 0.0) for hh in range(HG_H)], axis=0)
        sc = _dot_nt(qd[rs].astype(BF16), kblk.astype(BF16))
        sc = jnp.where(hgmask, sc, 0.0)
        vstk = jnp.concatenate([jnp.where(head_lane[hh], vi_g, 0.0) for hh in range(HG_H)], axis=0)
        oi_parts.append(_dot(sc.astype(BF16), vstk.astype(BF16)))
    o_intra = jnp.concatenate(oi_parts, axis=0)

    headblk_bf = headblk_ref[...]
    headblk = headblk_bf.astype(F32) > 0.5

    def hg_batch(bi, carry):
        st = hst_ref[bi]
        for cc in range(TT // HG_L):
            r0 = pl.multiple_of(bi * TT + cc * HG_L, HG_L)
            rows = pl.ds(r0, HG_L)
            hoi_ref[rows, :] = _dot_nt(qd_ref[rows, :], st.astype(BF16))
            dst = _dot_tn(hv_ref[rows, :], ke_ref[rows, :])
            st = dec_ref[pl.ds(r0, 1), :] * st + jnp.where(headblk, dst, 0.0)
        hst_ref[bi] = st
        return carry

    lax.fori_loop(0, NB, hg_batch, 0, unroll=4)

    o = o_intra + hoi_ref[...]
    o2 = o * o
    o2_hi = o2.astype(BF16)
    o2_lo = (o2 - o2_hi.astype(F32)).astype(BF16)
    ms_h = (_dot(o2_hi, headblk_bf) + _dot(o2_lo, headblk_bf)) * (1.0 / HG_DK)
    o_hg = o * lax.rsqrt(ms_h + EPS) * hgng_ref[...] * (zg * _sigmoid(zg))

    u = _dot(h, win_ref[:, _O_SU:_O_SU + S5_W])
    u_tb = _dot(perm_ref[...], u.astype(BF16)).astype(BF16)
    bu_ref[...] = _dot(u_tb, bblk_ref[...]).reshape(TT, NB, 2 * S5_N)
    ar = s5a_ref[0:1, :]
    ai = s5a_ref[1:2, :]

    def s5_step(t, carry):
        xr, xi = carry
        but = bu_ref[t]
        nr = ar * xr - ai * xi + but[:, :S5_N]
        ni = ar * xi + ai * xr + but[:, S5_N:]
        bu_ref[t] = jnp.concatenate([nr, ni], axis=-1)
        return nr, ni

    x0 = s5st_ref[...]
    xr, xi = lax.fori_loop(0, TT, s5_step, (x0[:, :S5_N], x0[:, S5_N:]), unroll=8)
    s5st_ref[...] = jnp.concatenate([xr, xi], axis=-1)
    xs = bu_ref[...].reshape(R, 2 * S5_N)
    y_tb = _dot(xs.astype(BF16), cblk_ref[...])
    y = _sel_dot(permt_ref[...], y_tb) + s5d_ref[...] * u
    y = 0.5 * y * (1.0 + jnp.tanh(math.sqrt(2.0 / math.pi) * (y + 0.044715 * (y * y * y))))
    o_s5 = y * _sigmoid(_dot(y.astype(BF16), wglu_ref[...]))

    xm = _dot(h, win_ref[:, _O_MX:_O_MX + ML_W])
    z = _dot(h, win_ref[:, _O_MZ:_O_MZ + ML_W])
    prev = prev_ref[...]
    tcol = lax.broadcasted_iota(jnp.int32, (R, 1), 0) % TT
    conv = xm * convw_ref[CONV_K - 1:CONV_K, :]
    for k in range(1, CONV_K):
        shifted = jnp.where(tcol >= k, pltpu.roll(xm, k, 0), pltpu.roll(prev, k + R - TT, 0))
        conv = conv + shifted * convw_ref[CONV_K - 1 - k:CONV_K - k, :]
    conv = conv + convb_ref[...]
    prev_ref[...] = xm
    xc = conv * _sigmoid(conv)
    xcb = xc.astype(BF16)
    xmb = xm.astype(BF16)
    q_hs, k_hs, vt_hs = [], [], []
    for hh in range(ML_H):
        ls = slice(hh * ML_DH, (hh + 1) * ML_DH)
        q_hs.append(_dot(xcb[:, ls], wq_ref[hh]))
        k_hs.append(_dot(xcb[:, ls], wk_ref[hh]) * (ML_DH ** -0.5))
        vt_hs.append(_dot_nt(wvt_ref[hh], xmb[:, ls]))
    q_all = jnp.concatenate(q_hs, axis=-1).astype(BF16)
    k_all = jnp.concatenate(k_hs, axis=-1).astype(BF16)
    vt_all = jnp.concatenate(vt_hs, axis=0).astype(BF16)

    g = (_dot_nt(wgq_ref[...], q_all) + _dot_nt(wgk_ref[...], k_all) + _dot(wgv_ref[...], vt_all)
         + bg_ref[...])
    b16 = _sel_dot_r(_log_sigmoid(g), l64t_ref[...])
    ig8 = g[0:8]
    b8 = b16[8:16]
    v8 = ig8 - b8
    tlane = lax.broadcasted_iota(jnp.int32, (1, R), 1) % TT
    cm = v8
    for sh in (1, 2, 4, 8, 16, 32):
        cm = jnp.maximum(cm, jnp.where(tlane >= sh, pltpu.roll(cm, sh, 1), -jnp.inf))
    sb = _sel_dot_r(jnp.concatenate([cm, b8], axis=0), selall_ref[...])
    vmax_sb = sb[0:8]
    glast_sb = sb[8:16]
    m_prev_sb = msb_ref[...]
    m_loc_sb = glast_sb + vmax_sb
    m_new_sb = jnp.maximum(glast_sb + m_prev_sb, m_loc_sb)
    sp_sb = jnp.exp(glast_sb + m_prev_sb - m_new_sb)
    sl_sb = jnp.exp(m_loc_sb - m_new_sb)
    msb_ref[...] = m_new_sb
    lane128 = lax.broadcasted_iota(jnp.int32, (1, 128), 1)
    first_half = lane128 < TT

    def per_time(sbarr):
        parts = [jnp.where(first_half, sbarr[:, (2 * p) * 128:(2 * p + 1) * 128],
                           sbarr[:, (2 * p + 1) * 128:(2 * p + 2) * 128]) for p in range(NB // 2)]
        return jnp.concatenate(parts, axis=-1)

    m_prev_t = per_time(m_prev_sb)
    mbig8 = jnp.maximum(cm, m_prev_t)
    scl8 = jnp.exp(m_prev_t - mbig8)
    en8 = jnp.exp(-(b8 + mbig8))
    w2_8 = per_time(sl_sb) * jnp.exp(v8 - per_time(vmax_sb))

    row16 = lax.broadcasted_iota(jnp.int32, (16, 1), 0)
    v_pieces = [piece.astype(F32) for piece in _split3(v8)]
    m_pieces = [piece.astype(F32) for piece in _split3(mbig8)]
    ones_hi = ((row16 >= 3) & (row16 < 6)).astype(F32)
    ones_lo = (row16 < 3).astype(F32)
    mlmask_t = mlmask_t_ref[...] > 0.5
    second_half = jnp.logical_not(first_half)
    n_rows = ((row16 == 0) & first_half) | ((row16 == 1) & second_half)

    for hh in range(ML_H):
        a_op = ones_hi
        b_op = ones_lo
        for i in range(3):
            a_op = a_op + jnp.where(row16 == i, jnp.broadcast_to(v_pieces[i][hh:hh + 1], (16, R)), 0.0)
            b_op = b_op - jnp.where(row16 == 3 + i, jnp.broadcast_to(m_pieces[i][hh:hh + 1], (16, R)), 0.0)
        a_op = a_op.astype(BF16)
        b_op = b_op.astype(BF16)
        ls = slice(hh * ML_DH, (hh + 1) * ML_DH)
        for p in range(NB // 2):
            tl = slice(p * 128, (p + 1) * 128)
            q_p = q_all[tl, ls]
            k_p = k_all[tl, ls]
            vt_p = vt_all[ls, tl]
            dm_t = jnp.where(mlmask_t, jnp.exp(_dot_tn(a_op[:, tl], b_op[:, tl])), 0.0)
            w_t = _dot_nt(k_p, q_p) * dm_t
            den_r = jnp.sum(w_t, axis=0, keepdims=True)
            num_t = _dot(vt_p, w_t.astype(BF16))
            i0 = (2 * p) * ML_H + hh
            i1 = i0 + ML_H
            ct0 = ct_ref[i0]
            ct1 = ct_ref[i1]
            inter_t = jnp.where(first_half, _dot_nt(ct0.astype(BF16), q_p), _dot_nt(ct1.astype(BF16), q_p))
            n_old = n_ref[p * ML_H + hh]
            qn = _dot_nt(n_old.astype(BF16), q_p)
            qn_r = jnp.where(first_half, qn[0:1], qn[1:2])
            scl_r = scl8[hh:hh + 1, tl]
            den = den_r + scl_r * qn_r
            h_t = (num_t + scl_r * inter_t) / jnp.maximum(jnp.abs(den), en8[hh:hh + 1, tl])
            hn_t = h_t * lax.rsqrt(jnp.mean(h_t * h_t, axis=0, keepdims=True) + EPS)
            mh_ref[tl, ls] = hn_t.T
            w2_r = w2_8[hh:hh + 1, tl]
            vtw = vt_p.astype(F32) * w2_r
            dct0 = _dot(jnp.where(first_half, vtw, 0.0).astype(BF16), k_p)
            dct1 = _dot(jnp.where(second_half, vtw, 0.0).astype(BF16), k_p)
            dn = _dot(jnp.where(n_rows, jnp.broadcast_to(w2_r, (16, 128)), 0.0).astype(BF16), k_p)
            sp0 = sp_sb[hh:hh + 1, (2 * p) * 128:(2 * p + 1) * 128]
            sp1 = sp_sb[hh:hh + 1, (2 * p + 1) * 128:(2 * p + 2) * 128]
            ct_ref[i0] = sp0 * ct0 + dct0
            ct_ref[i1] = sp1 * ct1 + dct1
            n_ref[p * ML_H + hh] = jnp.where(row16 == 0, sp0, sp1) * n_old + dn
    o_ml = (mh_ref[...] * mlng_ref[...] + mlskip_ref[...] * xc) * (z * _sigmoid(z))

    mixcat = jnp.concatenate([o_hg, o_s5, o_ml], axis=-1).astype(BF16)
    out = x + _dot(mixcat, wout_ref[...])
    o_ref[...] = out.reshape(NB, TT, D_MODEL)


def _ffn_kernel(x_ref, g_ref, wg_ref, wu_ref, wd_ref, gfin_ref, o_ref, *, final_norm):
    x = x_ref[...]
    h = (x * _rms_scale(x) * g_ref[...]).astype(BF16)
    acc = x
    for c in range(D_FF // FFN_TF):
        cs = slice(c * FFN_TF, (c + 1) * FFN_TF)
        gate = _dot(h, wg_ref[:, cs])
        up = _dot(h, wu_ref[:, cs])
        act = (gate * _sigmoid(gate) * up).astype(BF16)
        acc = acc + _dot(act, wd_ref[cs, :])
    if final_norm:
        acc = acc * _rms_scale(acc) * gfin_ref[...]
    o_ref[...] = acc


def _np_masks():
    r = np.arange(R)
    perm = np.zeros((R, R), np.float32)
    perm[(r % TT) * NB + (r // TT), r] = 1.0
    i256 = np.arange(256)
    lc = ((i256[:, None] // HG_L) == (i256[None, :] // HG_L)) & (i256[None, :] <= i256[:, None])
    i128 = np.arange(128)
    hg_causal = ((i128[:, None] // HG_L) == (i128[None, :] // HG_L)) & (i128[None, :] <= i128[:, None])
    hgmask = np.tile(hg_causal, (1, HG_H))
    headblk = (i256[:, None] // HG_DK) == (i256[None, :] // HG_DK)
    l64t = ((r[:, None] // TT) == (r[None, :] // TT)) & (r[:, None] <= r[None, :])
    mlmask_t = ((i128[:, None] // TT) == (i128[None, :] // TT)) & (i128[:, None] <= i128[None, :])
    selall = np.zeros((R, NB * 128), np.float32)
    for bb in range(NB):
        selall[bb * TT + TT - 1, bb * 128:(bb + 1) * 128] = 1.0
    return dict(perm=perm, permt=perm.T.copy(), lc=lc.astype(np.float32), hgmask=hgmask.astype(np.float32),
                headblk=headblk.astype(np.float32), l64t=l64t.astype(np.float32),
                mlmask_t=mlmask_t.astype(np.float32), selall=selall)


_MASKS = _np_masks()


def _const_spec(shape):
    nd = len(shape)
    return pl.BlockSpec(shape, lambda *_: (0,) * nd, pipeline_mode=pl.Buffered(1))


def _pad_rows(a, rows):
    return jnp.concatenate([a, jnp.zeros((rows - a.shape[0],) + a.shape[1:], a.dtype)], axis=0)


def _s5_params(lam_re, lam_im, log_dt, b_re, b_im, c_re, c_im):
    lre = lam_re.astype(F32)
    lim = lam_im.astype(F32)
    dt = jnp.exp(log_dt.astype(F32))[:, None]
    mag = jnp.exp(lre * dt)
    ang = lim * dt
    ar, ai = mag * jnp.cos(ang), mag * jnp.sin(ang)
    den = lre * lre + lim * lim
    nr, ni = ar - 1.0, ai
    cr = (nr * lre + ni * lim) / den
    ci = (ni * lre - nr * lim) / den
    bbr = cr[..., None] * b_re - ci[..., None] * b_im
    bbi = cr[..., None] * b_im + ci[..., None] * b_re
    eye = jnp.eye(S5_G, dtype=F32)
    blk_r = jnp.einsum('gpc,gk->gckp', bbr, eye).reshape(S5_W, S5_N)
    blk_i = jnp.einsum('gpc,gk->gckp', bbi, eye).reshape(S5_W, S5_N)
    bblk = jnp.concatenate([blk_r, blk_i], axis=1).astype(BF16)
    cb_r = jnp.einsum('gcp,gk->kpgc', c_re, eye).reshape(S5_N, S5_W)
    cb_i = jnp.einsum('gcp,gk->kpgc', c_im, eye).reshape(S5_N, S5_W)
    cblk = jnp.concatenate([cb_r, -cb_i], axis=0).astype(BF16)
    a = _pad_rows(jnp.stack([ar.reshape(S5_N), ai.reshape(S5_N)], axis=0), 8)
    return bblk, cblk, a


def _mix_layer(x, gmix, w_in, lb, hg_ng, s5, s5_d, w_glu, conv_w, conv_b, wq, wk, wv, w_gates, b_gates,
               ml_ng, ml_skip, w_out):
    B, S, _ = x.shape
    assert B % NB == 0 and S % TT == 0
    mk = _MASKS
    hgc = _pad_rows(jnp.stack([jnp.log(lb), jnp.log1p(-lb), 1.0 - lb], axis=0), 8)
    bblk, cblk, s5a = s5
    zrow = jnp.zeros((8 - ML_H, 3 * ML_W), F32)
    wg_rows = jnp.concatenate([w_gates[:, :ML_H].T, zrow, w_gates[:, ML_H:].T, zrow], axis=0).astype(BF16)
    zb = jnp.zeros((8 - ML_H,), F32)
    bg = jnp.broadcast_to(jnp.concatenate([b_gates[:ML_H], zb, b_gates[ML_H:], zb])[:, None], (16, R))
    consts = [
        gmix[None, :], w_in.astype(BF16),
        hgc, hg_ng[None, :], jnp.asarray(mk['lc'], BF16), jnp.asarray(mk['hgmask'], F32),
        jnp.asarray(mk['headblk'], BF16),
        jnp.asarray(mk['perm'], BF16), jnp.asarray(mk['permt'], BF16), bblk, cblk, s5a, s5_d[None, :],
        w_glu.astype(BF16),
        _pad_rows(conv_w, 8), conv_b[None, :], wq.astype(BF16), wk.astype(BF16),
        jnp.swapaxes(wv, 1, 2).astype(BF16),
        wg_rows[:, :ML_W], wg_rows[:, ML_W:2 * ML_W], wg_rows[:, 2 * ML_W:], bg,
        jnp.asarray(mk['l64t'], BF16), jnp.asarray(mk['selall'], BF16), jnp.asarray(mk['mlmask_t'], F32),
        ml_ng[None, :], ml_skip[None, :], w_out.astype(BF16),
    ]
    xspec = pl.BlockSpec((NB, TT, D_MODEL), lambda bi, si: (bi, si, 0))
    scratch = [
        pltpu.VMEM((NB, HG_W, HG_W), F32),
        pltpu.VMEM((NB, 2 * S5_N), F32),
        pltpu.VMEM((TT, NB, 2 * S5_N), F32),
        pltpu.VMEM((NB * ML_H, ML_DH, ML_DH), F32),
        pltpu.VMEM((NB // 2 * ML_H, 16, ML_DH), F32),
        pltpu.VMEM((8, NB * 128), F32),
        pltpu.VMEM((R, ML_W), F32),
        pltpu.VMEM((R, HG_W), BF16),
        pltpu.VMEM((R, HG_W), BF16),
        pltpu.VMEM((R, HG_W), BF16),
        pltpu.VMEM((R, HG_W), F32),
        pltpu.VMEM((R, HG_W), F32),
        pltpu.VMEM((R, ML_W), F32),
    ]
    return pl.pallas_call(
        _mix_kernel,
        grid=(B // NB, S // TT),
        in_specs=[xspec] + [_const_spec(c.shape) for c in consts],
        out_specs=xspec,
        out_shape=jax.ShapeDtypeStruct(x.shape, F32),
        scratch_shapes=scratch,
        compiler_params=pltpu.CompilerParams(
            dimension_semantics=("parallel", "arbitrary"), vmem_limit_bytes=VMEM_LIMIT),
        name="mixer",
    )(x, *consts)


def _ffn_layer(x2, g, wg, wu, wd, gfin, final_norm):
    T = x2.shape[0]
    assert T % FFN_TM == 0
    consts = [g[None, :], wg.astype(BF16), wu.astype(BF16), wd.astype(BF16), gfin[None, :]]
    xspec = pl.BlockSpec((FFN_TM, D_MODEL), lambda i: (i, 0))
    return pl.pallas_call(
        functools.partial(_ffn_kernel, final_norm=final_norm),
        grid=(T // FFN_TM,),
        in_specs=[xspec] + [_const_spec(c.shape) for c in consts],
        out_specs=xspec,
        out_shape=jax.ShapeDtypeStruct(x2.shape, F32),
        compiler_params=pltpu.CompilerParams(
            dimension_semantics=("parallel",), vmem_limit_bytes=VMEM_LIMIT),
        name="ffn",
    )(x2, *consts)


def kernel(x, norm_mix_g, w_in, hgrn_lb_raw, hgrn_norm_g, s5_lambda_re, s5_lambda_im, s5_log_dt, s5_b_re, s5_b_im, s5_c_re, s5_c_im, s5_d, s5_w_glu, ml_conv_w, ml_conv_b, ml_wq, ml_wk, ml_wv, ml_w_gates, ml_b_gates, ml_norm_g, ml_skip, w_out, norm_ffn_g, ffn_w_gate, ffn_w_up, ffn_w_down, norm_final_g):
    B, S, D = x.shape
    depth = w_in.shape[0]
    lb_cs = jnp.cumsum(jax.nn.softmax(hgrn_lb_raw.astype(F32), axis=0), axis=0)
    lb_all = lb_cs - lb_cs[0]
    for l in range(depth):
        s5 = _s5_params(s5_lambda_re[l], s5_lambda_im[l], s5_log_dt[l], s5_b_re[l], s5_b_im[l],
                        s5_c_re[l], s5_c_im[l])
        x = _mix_layer(x, norm_mix_g[l], w_in[l], lb_all[l], hgrn_norm_g[l], s5, s5_d[l], s5_w_glu[l],
                       ml_conv_w[l], ml_conv_b[l], ml_wq[l], ml_wk[l], ml_wv[l], ml_w_gates[l],
                       ml_b_gates[l], ml_norm_g[l], ml_skip[l], w_out[l])
        x2 = _ffn_layer(x.reshape(B * S, D), norm_ffn_g[l], ffn_w_gate[l], ffn_w_up[l], ffn_w_down[l],
                        norm_final_g, final_norm=(l == depth - 1))
        x = x2.reshape(B, S, D)
    return x
```

```python
import functools
import math

import numpy as np
import jax
import jax.numpy as jnp
from jax import lax
from jax.experimental import pallas as pl
from jax.experimental.pallas import tpu as pltpu

F32 = jnp.float32
BF16 = jnp.bfloat16

D_MODEL = 1024
HG_W = 256
HG_H = 4
HG_DK = HG_W // HG_H
HG_L = 32
S5_W = 256
S5_G = 16
S5_GC = 16
S5_P = 64
S5_N = S5_G * S5_P
ML_W = 512
ML_H = 4
ML_DH = ML_W // ML_H
CONV_K = 4
D_FF = 2816
D_IN = 4 * HG_W + S5_W + 2 * ML_W
EPS = 1e-6

NB = 8
TT = 64
R = NB * TT
FFN_TM = 512
FFN_TF = 256
S5_SCAN_STAGE = 32
VMEM_LIMIT = 58 * 1024 * 1024

_O_HQ, _O_HF, _O_HI, _O_HG, _O_SU, _O_MX, _O_MZ = 0, 256, 512, 768, 1024, 1280, 1792


def _dot(a, b):
    return jnp.dot(a, b, preferred_element_type=F32)


def _dot_nt(a, b):
    return lax.dot_general(a, b, (((1,), (1,)), ((), ())), preferred_element_type=F32)


def _dot_tn(a, b):
    return lax.dot_general(a, b, (((0,), (0,)), ((), ())), preferred_element_type=F32)


def _split3(x):
    hi = x.astype(BF16)
    r1 = x - hi.astype(F32)
    mid = r1.astype(BF16)
    lo = (r1 - mid.astype(F32)).astype(BF16)
    return hi, mid, lo


def _sel_dot(m01, x):
    hi, mid, lo = _split3(x)
    return _dot(m01, hi) + _dot(m01, mid) + _dot(m01, lo)


def _sel_dot_r(x, m01):
    hi, mid, lo = _split3(x)
    return _dot(hi, m01) + _dot(mid, m01) + _dot(lo, m01)


def _sigmoid(v):
    return 1.0 / (1.0 + jnp.exp(-v))


def _log_sigmoid(v):
    return jnp.minimum(v, 0.0) - jnp.log1p(jnp.exp(-jnp.abs(v)))


def _rms_scale(v):
    return lax.rsqrt(jnp.mean(v * v, axis=-1, keepdims=True) + EPS)


def _interleave(stage_generators):
    live = list(stage_generators)
    while live:
        for g in list(live):
            try:
                next(g)
            except StopIteration:
                live.remove(g)


def _mix_kernel(x_ref, gmix_ref, win_ref,
                hgc_ref, hgng_ref, lc_ref, hgmask_ref, headblk_ref,
                perm_ref, permt_ref, bblk_ref, cblk_ref, s5a_ref, s5d_ref, wglu_ref,
                convw_ref, convb_ref, wqk_ref, wvt_ref, wgq_ref, wgk_ref, wgv_ref, bg_ref,
                l64t_ref, selall_ref, mlmask_t_ref, mlng_ref, mlskip_ref, wout_ref,
                o_ref,
                hst_ref, s5st_ref, bu_ref, ct_ref, n_ref, msb_ref, prev_ref, mh_ref):
    si = pl.program_id(1)

    @pl.when(si == 0)
    def _reset_state():
        hst_ref[...] = jnp.zeros_like(hst_ref)
        s5st_ref[...] = jnp.zeros_like(s5st_ref)
        ct_ref[...] = jnp.zeros_like(ct_ref)
        n_ref[...] = jnp.zeros_like(n_ref)
        msb_ref[...] = jnp.zeros_like(msb_ref)
        prev_ref[...] = jnp.zeros_like(prev_ref)

    x = x_ref[...].reshape(R, D_MODEL)
    h = (x * _rms_scale(x) * gmix_ref[...]).astype(BF16)
    res = {}

    def hgrn():
        zq = _dot(h, win_ref[:, _O_HQ:_O_HQ + HG_W])
        zf = _dot(h, win_ref[:, _O_HF:_O_HF + HG_W])
        vi = _dot(h, win_ref[:, _O_HI:_O_HI + HG_W])
        zg = _dot(h, win_ref[:, _O_HG:_O_HG + HG_W])
        yield
        q = zq * _sigmoid(zq)
        log_lb = hgc_ref[0:1, :]
        log_1mlb = hgc_ref[1:2, :]
        one_mlb = hgc_ref[2:3, :]
        e = jnp.exp(-jnp.abs(zf))
        rcp = 1.0 / (1.0 + e)
        sig_neg = jnp.where(zf >= 0.0, e * rcp, rcp)
        a2 = log_1mlb + (jnp.minimum(zf, 0.0) - jnp.log1p(e))
        a1 = jnp.broadcast_to(log_lb, a2.shape)
        logf = jnp.maximum(a1, a2) + jnp.log1p(jnp.exp(-jnp.abs(a1 - a2)))
        kk = one_mlb * sig_neg
        lc = lc_ref[...]
        half = R // 2
        b = jnp.concatenate([_sel_dot(lc, logf[:half]), _sel_dot(lc, logf[half:])], axis=0)
        yield
        b3 = b.reshape(R // HG_L, HG_L, HG_W)
        blast = jnp.broadcast_to(b3[:, HG_L - 1:HG_L, :], b3.shape).reshape(R, HG_W)
        qd = (q * jnp.exp(b)).astype(BF16)
        ki = kk * jnp.exp(-b)
        ke = (kk * jnp.exp(blast - b)).astype(BF16)
        dec = jnp.exp(blast)
        vib = vi.astype(BF16)
        lane = lax.broadcasted_iota(jnp.int32, (1, HG_W), 1)
        head_lane = [(lane // HG_DK) == hh for hh in range(HG_H)]
        hgmask = hgmask_ref[...] > 0.5
        GR = 128
        scs = []
        for g in range(R // GR):
            rs = slice(g * GR, (g + 1) * GR)
            ki_g = ki[rs]
            kblk = jnp.concatenate([jnp.where(head_lane[hh], ki_g, 0.0) for hh in range(HG_H)], axis=0)
            scs.append(_dot_nt(qd[rs], kblk.astype(BF16)))
        yield
        oi_parts = []
        for g in range(R // GR):
            rs = slice(g * GR, (g + 1) * GR)
            vi_g = vi[rs]
            sc = jnp.where(hgmask, scs[g], 0.0)
            vstk = jnp.concatenate([jnp.where(head_lane[hh], vi_g, 0.0) for hh in range(HG_H)], axis=0)
            oi_parts.append(_dot(sc.astype(BF16), vstk.astype(BF16)))
        o_intra = jnp.concatenate(oi_parts, axis=0)
        yield
        headblk_bf = headblk_ref[...]
        headblk = headblk_bf.astype(F32) > 0.5
        sts = [hst_ref[bi] for bi in range(NB)]
        hoi = [[None] * (TT // HG_L) for _ in range(NB)]
        for cc in range(TT // HG_L):
            dsts = []
            for bi in range(NB):
                r0 = bi * TT + cc * HG_L
                rows = slice(r0, r0 + HG_L)
                hoi[bi][cc] = _dot_nt(qd[rows], sts[bi].astype(BF16))
                dsts.append(_dot_tn(vib[rows], ke[rows]))
            yield
            for bi in range(NB):
                r0 = bi * TT + cc * HG_L
                sts[bi] = dec[r0:r0 + 1, :] * sts[bi] + jnp.where(headblk, dsts[bi], 0.0)
        for bi in range(NB):
            hst_ref[bi] = sts[bi]
        o = o_intra + jnp.concatenate([part for per_b in hoi for part in per_b], axis=0)
        o2 = o * o
        o2_hi = o2.astype(BF16)
        o2_lo = (o2 - o2_hi.astype(F32)).astype(BF16)
        ms_h = (_dot(o2_hi, headblk_bf) + _dot(o2_lo, headblk_bf)) * (1.0 / HG_DK)
        yield
        o_hg = o * lax.rsqrt(ms_h + EPS) * hgng_ref[...] * (zg * _sigmoid(zg))
        res['hg'] = _dot(o_hg.astype(BF16), wout_ref[0:HG_W, :])

    def s5():
        u = _dot(h, win_ref[:, _O_SU:_O_SU + S5_W])
        yield
        u_tb = _dot(perm_ref[...], u.astype(BF16)).astype(BF16)
        yield
        bu_ref[...] = _dot(u_tb, bblk_ref[...]).reshape(TT, NB, 2 * S5_N)
        yield
        ar = s5a_ref[0:1, :]
        ai = s5a_ref[1:2, :]
        x0 = s5st_ref[...]
        xr, xi = x0[:, :S5_N], x0[:, S5_N:]
        for t in range(TT):
            but = bu_ref[t]
            xr, xi = ar * xr - ai * xi + but[:, :S5_N], ar * xi + ai * xr + but[:, S5_N:]
            bu_ref[t] = jnp.concatenate([xr, xi], axis=-1)
            if t % S5_SCAN_STAGE == S5_SCAN_STAGE - 1:
                yield
        s5st_ref[...] = jnp.concatenate([xr, xi], axis=-1)
        xs = bu_ref[...].reshape(R, 2 * S5_N)
        y_tb = _dot(xs.astype(BF16), cblk_ref[...])
        yield
        y_hi = y_tb.astype(BF16)
        y_lo = (y_tb - y_hi.astype(F32)).astype(BF16)
        permt = permt_ref[...]
        y = _dot(permt, y_hi) + _dot(permt, y_lo) + s5d_ref[...] * u
        yield
        y = 0.5 * y * (1.0 + jnp.tanh(math.sqrt(2.0 / math.pi) * (y + 0.044715 * (y * y * y))))
        glu = _dot(y.astype(BF16), wglu_ref[...])
        yield
        o_s5 = y * _sigmoid(glu)
        res['s5'] = _dot(o_s5.astype(BF16), wout_ref[HG_W:HG_W + S5_W, :])

    def mlstm():
        xm = _dot(h, win_ref[:, _O_MX:_O_MX + ML_W])
        z = _dot(h, win_ref[:, _O_MZ:_O_MZ + ML_W])
        yield
        prev = prev_ref[...]
        tcol = lax.broadcasted_iota(jnp.int32, (R, 1), 0) % TT
        conv = xm * convw_ref[CONV_K - 1:CONV_K, :]
        for k in range(1, CONV_K):
            shifted = jnp.where(tcol >= k, pltpu.roll(xm, k, 0), pltpu.roll(prev, k + R - TT, 0))
            conv = conv + shifted * convw_ref[CONV_K - 1 - k:CONV_K - k, :]
        conv = conv + convb_ref[...]
        prev_ref[...] = xm
        xc = conv * _sigmoid(conv)
        xcb = xc.astype(BF16)
        xmb = xm.astype(BF16)
        q_hs, k_hs, vt_hs = [], [], []
        for hh in range(ML_H):
            ls = slice(hh * ML_DH, (hh + 1) * ML_DH)
            qk = _dot(xcb[:, ls], wqk_ref[hh])
            q_hs.append(qk[:, :ML_DH])
            k_hs.append(qk[:, ML_DH:] * (ML_DH ** -0.5))
            vt_hs.append(_dot_nt(wvt_ref[hh], xmb[:, ls]))
        yield
        q_all = jnp.concatenate(q_hs, axis=-1).astype(BF16)
        k_all = jnp.concatenate(k_hs, axis=-1).astype(BF16)
        vt_all = jnp.concatenate(vt_hs, axis=0).astype(BF16)
        g = (_dot_nt(wgq_ref[...], q_all) + _dot_nt(wgk_ref[...], k_all) + _dot(wgv_ref[...], vt_all)
             + bg_ref[...])
        yield
        b16 = _sel_dot_r(_log_sigmoid(g), l64t_ref[...])
        yield
        ig8 = g[0:8]
        b8 = b16[8:16]
        v8 = ig8 - b8
        tlane = lax.broadcasted_iota(jnp.int32, (1, R), 1) % TT
        cm = v8
        for sh in (1, 2, 4, 8, 16, 32):
            cm = jnp.maximum(cm, jnp.where(tlane >= sh, pltpu.roll(cm, sh, 1), -jnp.inf))
        yield
        sb = _sel_dot_r(jnp.concatenate([cm, b8], axis=0), selall_ref[...])
        yield
        vmax_sb = sb[0:8]
        glast_sb = sb[8:16]
        m_prev_sb = msb_ref[...]
        m_loc_sb = glast_sb + vmax_sb
        m_new_sb = jnp.maximum(glast_sb + m_prev_sb, m_loc_sb)
        sp_sb = jnp.exp(glast_sb + m_prev_sb - m_new_sb)
        sl_sb = jnp.exp(m_loc_sb - m_new_sb)
        msb_ref[...] = m_new_sb
        lane128 = lax.broadcasted_iota(jnp.int32, (1, 128), 1)
        first_half = lane128 < TT

        def per_time(sbarr):
            parts = [jnp.where(first_half, sbarr[:, (2 * p) * 128:(2 * p + 1) * 128],
                               sbarr[:, (2 * p + 1) * 128:(2 * p + 2) * 128]) for p in range(NB // 2)]
            return jnp.concatenate(parts, axis=-1)

        m_prev_t = per_time(m_prev_sb)
        mbig8 = jnp.maximum(cm, m_prev_t)
        scl8 = jnp.exp(m_prev_t - mbig8)
        en8 = jnp.exp(-(b8 + mbig8))
        w2_8 = per_time(sl_sb) * jnp.exp(v8 - per_time(vmax_sb))

        row16 = lax.broadcasted_iota(jnp.int32, (16, 1), 0)
        v_pieces = [piece.astype(F32) for piece in _split3(v8)]
        m_pieces = [piece.astype(F32) for piece in _split3(mbig8)]
        ones_hi = ((row16 >= 3) & (row16 < 6)).astype(F32)
        ones_lo = (row16 < 3).astype(F32)
        mlmask_t = mlmask_t_ref[...] > 0.5
        second_half = jnp.logical_not(first_half)
        n_rows = ((row16 == 0) & first_half) | ((row16 == 1) & second_half)
        pairs = range(NB // 2)

        for hh in range(ML_H):
            a_op = ones_hi
            b_op = ones_lo
            for i in range(3):
                a_op = a_op + jnp.where(row16 == i, jnp.broadcast_to(v_pieces[i][hh:hh + 1], (16, R)), 0.0)
                b_op = b_op - jnp.where(row16 == 3 + i, jnp.broadcast_to(m_pieces[i][hh:hh + 1], (16, R)), 0.0)
            a_op = a_op.astype(BF16)
            b_op = b_op.astype(BF16)
            ls = slice(hh * ML_DH, (hh + 1) * ML_DH)
            tls = [slice(p * 128, (p + 1) * 128) for p in pairs]
            q_ps = [q_all[tl, ls] for tl in tls]
            k_ps = [k_all[tl, ls] for tl in tls]
            vt_ps = [vt_all[ls, tl] for tl in tls]
            logdms = [_dot_tn(a_op[:, tl], b_op[:, tl]) for tl in tls]
            sts = [_dot_nt(k_ps[p], q_ps[p]) for p in pairs]
            yield
            w_ts, num_ts, inter_ts, qns, cts, n_olds = [], [], [], [], [], []
            for p in pairs:
                w_t = sts[p] * jnp.where(mlmask_t, jnp.exp(logdms[p]), 0.0)
                i0 = (2 * p) * ML_H + hh
                ct0 = ct_ref[i0]
                ct1 = ct_ref[i0 + ML_H]
                n_old = n_ref[p * ML_H + hh]
                w_ts.append(w_t)
                num_ts.append(_dot(vt_ps[p], w_t.astype(BF16)))
                inter_ts.append(jnp.where(first_half, _dot_nt(ct0.astype(BF16), q_ps[p]),
                                          _dot_nt(ct1.astype(BF16), q_ps[p])))
                qns.append(_dot_nt(n_old.astype(BF16), q_ps[p]))
                cts.append((ct0, ct1))
                n_olds.append(n_old)
            yield
            for p in pairs:
                tl = tls[p]
                den_r = jnp.sum(w_ts[p], axis=0, keepdims=True)
                qn_r = jnp.where(first_half, qns[p][0:1], qns[p][1:2])
                scl_r = scl8[hh:hh + 1, tl]
                den = den_r + scl_r * qn_r
                h_t = (num_ts[p] + scl_r * inter_ts[p]) / jnp.maximum(jnp.abs(den), en8[hh:hh + 1, tl])
                hn_t = h_t * lax.rsqrt(jnp.mean(h_t * h_t, axis=0, keepdims=True) + EPS)
                mh_ref[tl, ls] = hn_t.T
                w2_r = w2_8[hh:hh + 1, tl]
                vtw = vt_ps[p].astype(F32) * w2_r
                dct0 = _dot(jnp.where(first_half, vtw, 0.0).astype(BF16), k_ps[p])
                dct1 = _dot(jnp.where(second_half, vtw, 0.0).astype(BF16), k_ps[p])
                dn = _dot(jnp.where(n_rows, jnp.broadcast_to(w2_r, (16, 128)), 0.0).astype(BF16), k_ps[p])
                sp0 = sp_sb[hh:hh + 1, (2 * p) * 128:(2 * p + 1) * 128]
                sp1 = sp_sb[hh:hh + 1, (2 * p + 1) * 128:(2 * p + 2) * 128]
                i0 = (2 * p) * ML_H + hh
                ct_ref[i0] = sp0 * cts[p][0] + dct0
                ct_ref[i0 + ML_H] = sp1 * cts[p][1] + dct1
                n_ref[p * ML_H + hh] = jnp.where(row16 == 0, sp0, sp1) * n_olds[p] + dn
            yield
        o_ml = (mh_ref[...] * mlng_ref[...] + mlskip_ref[...] * xc) * (z * _sigmoid(z))
        res['ml'] = _dot(o_ml.astype(BF16), wout_ref[HG_W + S5_W:, :])

    _interleave([s5(), mlstm(), hgrn()])

    out = x + (res['hg'] + res['s5'] + res['ml'])
    o_ref[...] = out.reshape(NB, TT, D_MODEL)


def _ffn_kernel(x_ref, g_ref, wg_ref, wu_ref, wd_ref, gfin_ref, o_ref, *, final_norm):
    x = x_ref[...]
    h = (x * _rms_scale(x) * g_ref[...]).astype(BF16)
    acc = x
    for c in range(D_FF // FFN_TF):
        cs = slice(c * FFN_TF, (c + 1) * FFN_TF)
        gate = _dot(h, wg_ref[:, cs])
        up = _dot(h, wu_ref[:, cs])
        act = (gate * _sigmoid(gate) * up).astype(BF16)
        acc = acc + _dot(act, wd_ref[cs, :])
    if final_norm:
        acc = acc * _rms_scale(acc) * gfin_ref[...]
    o_ref[...] = acc


def _np_masks():
    r = np.arange(R)
    perm = np.zeros((R, R), np.float32)
    perm[(r % TT) * NB + (r // TT), r] = 1.0
    i256 = np.arange(256)
    lc = ((i256[:, None] // HG_L) == (i256[None, :] // HG_L)) & (i256[None, :] <= i256[:, None])
    i128 = np.arange(128)
    hg_causal = ((i128[:, None] // HG_L) == (i128[None, :] // HG_L)) & (i128[None, :] <= i128[:, None])
    hgmask = np.tile(hg_causal, (1, HG_H))
    headblk = (i256[:, None] // HG_DK) == (i256[None, :] // HG_DK)
    l64t = ((r[:, None] // TT) == (r[None, :] // TT)) & (r[:, None] <= r[None, :])
    mlmask_t = ((i128[:, None] // TT) == (i128[None, :] // TT)) & (i128[:, None] <= i128[None, :])
    selall = np.zeros((R, NB * 128), np.float32)
    for bb in range(NB):
        selall[bb * TT + TT - 1, bb * 128:(bb + 1) * 128] = 1.0
    return dict(perm=perm, permt=perm.T.copy(), lc=lc.astype(np.float32), hgmask=hgmask.astype(np.float32),
                headblk=headblk.astype(np.float32), l64t=l64t.astype(np.float32),
                mlmask_t=mlmask_t.astype(np.float32), selall=selall)


_MASKS = _np_masks()


def _const_spec(shape):
    nd = len(shape)
    return pl.BlockSpec(shape, lambda *_: (0,) * nd, pipeline_mode=pl.Buffered(1))


def _pad_rows(a, rows):
    return jnp.concatenate([a, jnp.zeros((rows - a.shape[0],) + a.shape[1:], a.dtype)], axis=0)


def _s5_params(lam_re, lam_im, log_dt, b_re, b_im, c_re, c_im):
    lre = lam_re.astype(F32)
    lim = lam_im.astype(F32)
    dt = jnp.exp(log_dt.astype(F32))[:, None]
    mag = jnp.exp(lre * dt)
    ang = lim * dt
    ar, ai = mag * jnp.cos(ang), mag * jnp.sin(ang)
    den = lre * lre + lim * lim
    nr, ni = ar - 1.0, ai
    cr = (nr * lre + ni * lim) / den
    ci = (ni * lre - nr * lim) / den
    bbr = cr[..., None] * b_re - ci[..., None] * b_im
    bbi = cr[..., None] * b_im + ci[..., None] * b_re
    eye = jnp.eye(S5_G, dtype=F32)
    blk_r = jnp.einsum('gpc,gk->gckp', bbr, eye).reshape(S5_W, S5_N)
    blk_i = jnp.einsum('gpc,gk->gckp', bbi, eye).reshape(S5_W, S5_N)
    bblk = jnp.concatenate([blk_r, blk_i], axis=1).astype(BF16)
    cb_r = jnp.einsum('gcp,gk->kpgc', c_re, eye).reshape(S5_N, S5_W)
    cb_i = jnp.einsum('gcp,gk->kpgc', c_im, eye).reshape(S5_N, S5_W)
    cblk = jnp.concatenate([cb_r, -cb_i], axis=0).astype(BF16)
    a = _pad_rows(jnp.stack([ar.reshape(S5_N), ai.reshape(S5_N)], axis=0), 8)
    return bblk, cblk, a


def _mix_layer(x, gmix, w_in, lb, hg_ng, s5, s5_d, w_glu, conv_w, conv_b, wq, wk, wv, w_gates, b_gates,
               ml_ng, ml_skip, w_out):
    B, S, _ = x.shape
    assert B % NB == 0 and S % TT == 0
    mk = _MASKS
    hgc = _pad_rows(jnp.stack([jnp.log(lb), jnp.log1p(-lb), 1.0 - lb], axis=0), 8)
    bblk, cblk, s5a = s5
    zrow = jnp.zeros((8 - ML_H, 3 * ML_W), F32)
    wg_rows = jnp.concatenate([w_gates[:, :ML_H].T, zrow, w_gates[:, ML_H:].T, zrow], axis=0).astype(BF16)
    zb = jnp.zeros((8 - ML_H,), F32)
    bg = jnp.broadcast_to(jnp.concatenate([b_gates[:ML_H], zb, b_gates[ML_H:], zb])[:, None], (16, R))
    consts = [
        gmix[None, :], w_in.astype(BF16),
        hgc, hg_ng[None, :], jnp.asarray(mk['lc'], BF16), jnp.asarray(mk['hgmask'], F32),
        jnp.asarray(mk['headblk'], BF16),
        jnp.asarray(mk['perm'], BF16), jnp.asarray(mk['permt'], BF16), bblk, cblk, s5a, s5_d[None, :],
        w_glu.astype(BF16),
        _pad_rows(conv_w, 8), conv_b[None, :], jnp.concatenate([wq, wk], axis=-1).astype(BF16),
        jnp.swapaxes(wv, 1, 2).astype(BF16),
        wg_rows[:, :ML_W], wg_rows[:, ML_W:2 * ML_W], wg_rows[:, 2 * ML_W:], bg,
        jnp.asarray(mk['l64t'], BF16), jnp.asarray(mk['selall'], BF16), jnp.asarray(mk['mlmask_t'], F32),
        ml_ng[None, :], ml_skip[None, :], w_out.astype(BF16),
    ]
    xspec = pl.BlockSpec((NB, TT, D_MODEL), lambda bi, si: (bi, si, 0))
    scratch = [
        pltpu.VMEM((NB, HG_W, HG_W), F32),
        pltpu.VMEM((NB, 2 * S5_N), F32),
        pltpu.VMEM((TT, NB, 2 * S5_N), F32),
        pltpu.VMEM((NB * ML_H, ML_DH, ML_DH), F32),
        pltpu.VMEM((NB // 2 * ML_H, 16, ML_DH), F32),
        pltpu.VMEM((8, NB * 128), F32),
        pltpu.VMEM((R, ML_W), F32),
        pltpu.VMEM((R, ML_W), F32),
    ]
    return pl.pallas_call(
        _mix_kernel,
        grid=(B // NB, S // TT),
        in_specs=[xspec] + [_const_spec(c.shape) for c in consts],
        out_specs=xspec,
        out_shape=jax.ShapeDtypeStruct(x.shape, F32),
        scratch_shapes=scratch,
        compiler_params=pltpu.CompilerParams(
            dimension_semantics=("parallel", "arbitrary"), vmem_limit_bytes=VMEM_LIMIT),
        name="mixer",
    )(x, *consts)


def _ffn_layer(x2, g, wg, wu, wd, gfin, final_norm):
    T = x2.shape[0]
    assert T % FFN_TM == 0
    consts = [g[None, :], wg.astype(BF16), wu.astype(BF16), wd.astype(BF16), gfin[None, :]]
    xspec = pl.BlockSpec((FFN_TM, D_MODEL), lambda i: (i, 0))
    return pl.pallas_call(
        functools.partial(_ffn_kernel, final_norm=final_norm),
        grid=(T // FFN_TM,),
        in_specs=[xspec] + [_const_spec(c.shape) for c in consts],
        out_specs=xspec,
        out_shape=jax.ShapeDtypeStruct(x2.shape, F32),
        compiler_params=pltpu.CompilerParams(
            dimension_semantics=("parallel",), vmem_limit_bytes=VMEM_LIMIT),
        name="ffn",
    )(x2, *consts)


def kernel(x, norm_mix_g, w_in, hgrn_lb_raw, hgrn_norm_g, s5_lambda_re, s5_lambda_im, s5_log_dt, s5_b_re, s5_b_im, s5_c_re, s5_c_im, s5_d, s5_w_glu, ml_conv_w, ml_conv_b, ml_wq, ml_wk, ml_wv, ml_w_gates, ml_b_gates, ml_norm_g, ml_skip, w_out, norm_ffn_g, ffn_w_gate, ffn_w_up, ffn_w_down, norm_final_g):
    B, S, D = x.shape
    depth = w_in.shape[0]
    lb_cs = jnp.cumsum(jax.nn.softmax(hgrn_lb_raw.astype(F32), axis=0), axis=0)
    lb_all = lb_cs - lb_cs[0]
    for l in range(depth):
        s5 = _s5_params(s5_lambda_re[l], s5_lambda_im[l], s5_log_dt[l], s5_b_re[l], s5_b_im[l],
                        s5_c_re[l], s5_c_im[l])
        x = _mix_layer(x, norm_mix_g[l], w_in[l], lb_all[l], hgrn_norm_g[l], s5, s5_d[l], s5_w_glu[l],
                       ml_conv_w[l], ml_conv_b[l], ml_wq[l], ml_wk[l], ml_wv[l], ml_w_gates[l],
                       ml_b_gates[l], ml_norm_g[l], ml_skip[l], w_out[l])
        x2 = _ffn_layer(x.reshape(B * S, D), norm_ffn_g[l], ffn_w_gate[l], ffn_w_up[l], ffn_w_down[l],
                        norm_final_g, final_norm=(l == depth - 1))
        x = x2.reshape(B, S, D)
    return x
```

```python
import functools
import math

import numpy as np
import jax
import jax.numpy as jnp
from jax import lax
from jax.experimental import pallas as pl
from jax.experimental.pallas import tpu as pltpu

F32 = jnp.float32
BF16 = jnp.bfloat16

D_MODEL = 1024
HG_W = 256
HG_H = 4
HG_DK = HG_W // HG_H
HG_L = 32
S5_W = 256
S5_G = 16
S5_GC = 16
S5_P = 64
S5_N = S5_G * S5_P
ML_W = 512
ML_H = 4
ML_DH = ML_W // ML_H
CONV_K = 4
D_FF = 2816
D_IN = 4 * HG_W + S5_W + 2 * ML_W
EPS = 1e-6

NB = 8
TT = 64
R = NB * TT
FFN_TM = 1024
FFN_TF = 256
S5_SCAN_STAGE = 32
VMEM_LIMIT = 58 * 1024 * 1024

_O_HQ, _O_HF, _O_HI, _O_HG, _O_SU, _O_MX, _O_MZ = 0, 256, 512, 768, 1024, 1280, 1792


def _dot(a, b):
    return jnp.dot(a, b, preferred_element_type=F32)


def _dot_nt(a, b):
    return lax.dot_general(a, b, (((1,), (1,)), ((), ())), preferred_element_type=F32)


def _dot_tn(a, b):
    return lax.dot_general(a, b, (((0,), (0,)), ((), ())), preferred_element_type=F32)


def _split3(x):
    hi = x.astype(BF16)
    r1 = x - hi.astype(F32)
    mid = r1.astype(BF16)
    lo = (r1 - mid.astype(F32)).astype(BF16)
    return hi, mid, lo


def _sel_dot(m01, x):
    hi, mid, lo = _split3(x)
    return _dot(m01, hi) + _dot(m01, mid) + _dot(m01, lo)


def _silu(v):
    hv = 0.5 * v
    return hv * jnp.tanh(hv) + hv


def _log_sigmoid(v):
    return jnp.minimum(v, 0.0) - jnp.log1p(jnp.exp(-jnp.abs(v)))


def _rms_scale(v):
    return lax.rsqrt(jnp.mean(v * v, axis=-1, keepdims=True) + EPS)


def _interleave(stage_generators):
    live = list(stage_generators)
    while live:
        for g in list(live):
            try:
                next(g)
            except StopIteration:
                live.remove(g)


def _mix_kernel(x_ref, gmix_ref, win_ref,
                hgc_ref, hgng_ref, lc_ref, hgmask_ref, headblk_ref,
                bblk_ref, cblk_ref, s5a_ref, s5d_ref, wglu_ref,
                convw_ref, convb_ref, wqk_ref, wvt_ref, wgx_ref, wgm_ref, bg_ref,
                l128_ref, selpair_ref, mlmask_t_ref, mlng_ref, mlskip_ref, wout_ref,
                o_ref,
                hst_ref, s5st_ref, bu_ref, ct_ref, n_ref, msb_ref, prev_ref, mh_ref):
    si = pl.program_id(1)

    @pl.when(si == 0)
    def _reset_state():
        hst_ref[...] = jnp.zeros_like(hst_ref)
        s5st_ref[...] = jnp.zeros_like(s5st_ref)
        ct_ref[...] = jnp.zeros_like(ct_ref)
        n_ref[...] = jnp.zeros_like(n_ref)
        msb_ref[...] = jnp.zeros_like(msb_ref)
        prev_ref[...] = jnp.zeros_like(prev_ref)

    x = x_ref[...].reshape(R, D_MODEL)
    h = (x * _rms_scale(x) * gmix_ref[...]).astype(BF16)
    res = {}

    def hgrn():
        zq = _dot(h, win_ref[:, _O_HQ:_O_HQ + HG_W])
        zf = _dot(h, win_ref[:, _O_HF:_O_HF + HG_W])
        vi = _dot(h, win_ref[:, _O_HI:_O_HI + HG_W])
        zg = _dot(h, win_ref[:, _O_HG:_O_HG + HG_W])
        yield
        q = _silu(zq)
        log_lb = hgc_ref[0:1, :]
        log_1mlb = hgc_ref[1:2, :]
        one_mlb = hgc_ref[2:3, :]
        e = jnp.exp(-jnp.abs(zf))
        rcp = 1.0 / (1.0 + e)
        sig_neg = jnp.where(zf >= 0.0, e * rcp, rcp)
        a2 = log_1mlb + (jnp.minimum(zf, 0.0) - jnp.log1p(e))
        a1 = jnp.broadcast_to(log_lb, a2.shape)
        logf = jnp.maximum(a1, a2) + jnp.log1p(jnp.exp(-jnp.abs(a1 - a2)))
        kk = one_mlb * sig_neg
        lc = lc_ref[...]
        half = R // 2
        b = jnp.concatenate([_sel_dot(lc, logf[:half]), _sel_dot(lc, logf[half:])], axis=0)
        yield
        b3 = b.reshape(R // HG_L, HG_L, HG_W)
        blast = jnp.broadcast_to(b3[:, HG_L - 1:HG_L, :], b3.shape).reshape(R, HG_W)
        qd = (q * jnp.exp(b)).astype(BF16)
        ki = kk * jnp.exp(-b)
        ke = (kk * jnp.exp(blast - b)).astype(BF16)
        dec = jnp.exp(blast)
        vib = vi.astype(BF16)
        lane = lax.broadcasted_iota(jnp.int32, (1, HG_W), 1)
        head_lane = [(lane // HG_DK) == hh for hh in range(HG_H)]
        hgmask = hgmask_ref[...] > 0.5
        GR = 128
        scs = []
        for g in range(R // GR):
            rs = slice(g * GR, (g + 1) * GR)
            ki_g = ki[rs]
            kblk = jnp.concatenate([jnp.where(head_lane[hh], ki_g, 0.0) for hh in range(HG_H)], axis=0)
            scs.append(_dot_nt(qd[rs], kblk.astype(BF16)))
        yield
        oi_parts = []
        for g in range(R // GR):
            rs = slice(g * GR, (g + 1) * GR)
            vi_g = vi[rs]
            sc = jnp.where(hgmask, scs[g], 0.0)
            vstk = jnp.concatenate([jnp.where(head_lane[hh], vi_g, 0.0) for hh in range(HG_H)], axis=0)
            oi_parts.append(_dot(sc.astype(BF16), vstk.astype(BF16)))
        o_intra = jnp.concatenate(oi_parts, axis=0)
        yield
        headblk_bf = headblk_ref[...]
        headblk = headblk_bf.astype(F32) > 0.5
        sts = [hst_ref[bi] for bi in range(NB)]
        hoi = [[None] * (TT // HG_L) for _ in range(NB)]
        for cc in range(TT // HG_L):
            dsts = []
            for bi in range(NB):
                r0 = bi * TT + cc * HG_L
                rows = slice(r0, r0 + HG_L)
                hoi[bi][cc] = _dot_nt(qd[rows], sts[bi].astype(BF16))
                dsts.append(_dot_tn(vib[rows], ke[rows]))
            yield
            for bi in range(NB):
                r0 = bi * TT + cc * HG_L
                sts[bi] = dec[r0:r0 + 1, :] * sts[bi] + jnp.where(headblk, dsts[bi], 0.0)
        for bi in range(NB):
            hst_ref[bi] = sts[bi]
        o = o_intra + jnp.concatenate([part for per_b in hoi for part in per_b], axis=0)
        o2 = o * o
        o2_hi = o2.astype(BF16)
        o2_lo = (o2 - o2_hi.astype(F32)).astype(BF16)
        ms_h = (_dot(o2_hi, headblk_bf) + _dot(o2_lo, headblk_bf)) * (1.0 / HG_DK)
        yield
        o_hg = o * lax.rsqrt(ms_h + EPS) * hgng_ref[...] * _silu(zg)
        res['hg'] = _dot(o_hg.astype(BF16), wout_ref[0:HG_W, :])

    def s5():
        u = _dot(h, win_ref[:, _O_SU:_O_SU + S5_W])
        yield
        u_tb = jnp.swapaxes(u.reshape(NB, TT, S5_W), 0, 1).reshape(R, S5_W).astype(BF16)
        yield
        bu_ref[...] = _dot(u_tb, bblk_ref[...]).reshape(TT, NB, 2 * S5_N)
        yield
        ar = s5a_ref[0:1, :]
        ai = s5a_ref[1:2, :]
        x0 = s5st_ref[...]
        xr, xi = x0[:, :S5_N], x0[:, S5_N:]
        for t in range(TT):
            but = bu_ref[t]
            xr, xi = ar * xr - ai * xi + but[:, :S5_N], ar * xi + ai * xr + but[:, S5_N:]
            bu_ref[t] = jnp.concatenate([xr, xi], axis=-1)
            if t % S5_SCAN_STAGE == S5_SCAN_STAGE - 1:
                yield
        s5st_ref[...] = jnp.concatenate([xr, xi], axis=-1)
        xs = bu_ref[...].reshape(R, 2 * S5_N)
        y_tb = _dot(xs.astype(BF16), cblk_ref[...])
        yield
        y = jnp.swapaxes(y_tb.reshape(TT, NB, S5_W), 0, 1).reshape(R, S5_W) + s5d_ref[...] * u
        yield
        y = 0.5 * y * (1.0 + jnp.tanh(math.sqrt(2.0 / math.pi) * (y + 0.044715 * (y * y * y))))
        glu = _dot(y.astype(BF16), wglu_ref[...])
        yield
        o_s5 = y * (0.5 * jnp.tanh(0.5 * glu) + 0.5)
        res['s5'] = _dot(o_s5.astype(BF16), wout_ref[HG_W:HG_W + S5_W, :])

    def mlstm():
        xm = _dot(h, win_ref[:, _O_MX:_O_MX + ML_W])
        z = _dot(h, win_ref[:, _O_MZ:_O_MZ + ML_W])
        yield
        tail = prev_ref[...]
        xm3 = xm.reshape(NB, TT, ML_W)
        prev_ref[...] = xm3[:, TT - 8:, :].reshape(NB * 8, ML_W)
        t8 = lax.broadcasted_iota(jnp.int32, (1, 8, 1), 1)
        conv = xm * convw_ref[CONV_K - 1:CONV_K, :]
        for k in range(1, CONV_K):
            rolled = pltpu.roll(xm, k, 0).reshape(NB, TT, ML_W)
            tail_k = pltpu.roll(tail, NB * 8 - 8 + k, 0).reshape(NB, 8, ML_W)
            head = jnp.where(t8 >= k, rolled[:, 0:8, :], tail_k)
            shifted = jnp.concatenate([head, rolled[:, 8:, :]], axis=1).reshape(R, ML_W)
            conv = conv + shifted * convw_ref[CONV_K - 1 - k:CONV_K - k, :]
        conv = conv + convb_ref[...]
        xc = _silu(conv)
        xcb = xc.astype(BF16)
        xmb = xm.astype(BF16)
        q_hs, k_hs, vt_hs = [], [], []
        for hh in range(ML_H):
            ls = slice(hh * ML_DH, (hh + 1) * ML_DH)
            qk = _dot(xcb[:, ls], wqk_ref[hh])
            q_hs.append(qk[:, :ML_DH])
            k_hs.append(qk[:, ML_DH:] * (ML_DH ** -0.5))
            vt_hs.append(_dot_nt(wvt_ref[hh], xmb[:, ls]))
        yield
        q_all = jnp.concatenate(q_hs, axis=-1).astype(BF16)
        k_all = jnp.concatenate(k_hs, axis=-1).astype(BF16)
        vt_all = jnp.concatenate(vt_hs, axis=0).astype(BF16)
        g = _dot_nt(wgx_ref[...], xcb) + _dot_nt(wgm_ref[...], xmb) + bg_ref[...]
        yield
        pairs = range(NB // 2)
        tls = [slice(p * 128, (p + 1) * 128) for p in pairs]
        l128 = l128_ref[...]
        lsg3 = jnp.concatenate(_split3(_log_sigmoid(g)), axis=0)
        cs = [_dot(lsg3[:, tl], l128) for tl in tls]
        b16 = jnp.concatenate([c[0:16] + c[16:32] + c[32:48] for c in cs], axis=-1)
        yield
        ig8 = g[0:8]
        b8 = b16[8:16]
        v8 = ig8 - b8
        tlane = lax.broadcasted_iota(jnp.int32, (1, R), 1) % TT
        cm = v8
        for sh in (1, 2, 4, 8, 16, 32):
            cm = jnp.maximum(cm, jnp.where(tlane >= sh, pltpu.roll(cm, sh, 1), -jnp.inf))
        yield
        selpair = selpair_ref[...]
        cb3 = jnp.concatenate(_split3(jnp.concatenate([cm, b8], axis=0)), axis=0)
        ss = [_dot(cb3[:, tl], selpair) for tl in tls]
        sb = jnp.concatenate([c[0:16] + c[16:32] + c[32:48] for c in ss], axis=-1)
        yield
        vmax_sb = sb[0:8]
        glast_sb = sb[8:16]
        m_prev_sb = msb_ref[...]
        m_loc_sb = glast_sb + vmax_sb
        m_new_sb = jnp.maximum(glast_sb + m_prev_sb, m_loc_sb)
        sp_sb = jnp.exp(glast_sb + m_prev_sb - m_new_sb)
        sl_sb = jnp.exp(m_loc_sb - m_new_sb)
        msb_ref[...] = m_new_sb
        lane128 = lax.broadcasted_iota(jnp.int32, (1, 128), 1)
        first_half = lane128 < TT

        def per_time(sbarr):
            parts = [jnp.where(first_half, sbarr[:, (2 * p) * 128:(2 * p + 1) * 128],
                               sbarr[:, (2 * p + 1) * 128:(2 * p + 2) * 128]) for p in range(NB // 2)]
            return jnp.concatenate(parts, axis=-1)

        m_prev_t = per_time(m_prev_sb)
        mbig8 = jnp.maximum(cm, m_prev_t)
        scl8 = jnp.exp(m_prev_t - mbig8)
        en8 = jnp.exp(-(b8 + mbig8))
        w2_8 = per_time(sl_sb) * jnp.exp(v8 - per_time(vmax_sb))

        row16 = lax.broadcasted_iota(jnp.int32, (16, 1), 0)
        v_pieces = [piece.astype(F32) for piece in _split3(v8)]
        m_pieces = [piece.astype(F32) for piece in _split3(mbig8)]
        ones_hi = ((row16 >= 3) & (row16 < 6)).astype(F32)
        ones_lo = (row16 < 3).astype(F32)
        mlmask_t = mlmask_t_ref[...] > 0.5
        second_half = jnp.logical_not(first_half)
        n_rows = ((row16 == 0) & first_half) | ((row16 == 1) & second_half)

        for hh in range(ML_H):
            a_op = ones_hi
            b_op = ones_lo
            for i in range(3):
                a_op = a_op + jnp.where(row16 == i, jnp.broadcast_to(v_pieces[i][hh:hh + 1], (16, R)), 0.0)
                b_op = b_op - jnp.where(row16 == 3 + i, jnp.broadcast_to(m_pieces[i][hh:hh + 1], (16, R)), 0.0)
            a_op = a_op.astype(BF16)
            b_op = b_op.astype(BF16)
            ls = slice(hh * ML_DH, (hh + 1) * ML_DH)
            q_ps = [q_all[tl, ls] for tl in tls]
            k_ps = [k_all[tl, ls] for tl in tls]
            vt_ps = [vt_all[ls, tl] for tl in tls]
            logdms = [_dot_tn(a_op[:, tl], b_op[:, tl]) for tl in tls]
            sts = [_dot_nt(k_ps[p], q_ps[p]) for p in pairs]
            yield
            w_ts, num_ts, inter_ts, qns, cts, n_olds = [], [], [], [], [], []
            for p in pairs:
                w_t = sts[p] * jnp.where(mlmask_t, jnp.exp(logdms[p]), 0.0)
                i0 = (2 * p) * ML_H + hh
                ct0 = ct_ref[i0]
                ct1 = ct_ref[i0 + ML_H]
                n_old = n_ref[p * ML_H + hh]
                w_ts.append(w_t)
                num_ts.append(_dot(vt_ps[p], w_t.astype(BF16)))
                inter_ts.append(jnp.where(first_half, _dot_nt(ct0.astype(BF16), q_ps[p]),
                                          _dot_nt(ct1.astype(BF16), q_ps[p])))
                qns.append(_dot_nt(n_old.astype(BF16), q_ps[p]))
                cts.append((ct0, ct1))
                n_olds.append(n_old)
            yield
            for p in pairs:
                tl = tls[p]
                den_r = jnp.sum(w_ts[p], axis=0, keepdims=True)
                qn_r = jnp.where(first_half, qns[p][0:1], qns[p][1:2])
                scl_r = scl8[hh:hh + 1, tl]
                den = den_r + scl_r * qn_r
                h_t = (num_ts[p] + scl_r * inter_ts[p]) / jnp.maximum(jnp.abs(den), en8[hh:hh + 1, tl])
                hn_t = h_t * lax.rsqrt(jnp.mean(h_t * h_t, axis=0, keepdims=True) + EPS)
                mh_ref[tl, ls] = hn_t.T
                w2_r = w2_8[hh:hh + 1, tl]
                vtw = vt_ps[p].astype(F32) * w2_r
                dct0 = _dot(jnp.where(first_half, vtw, 0.0).astype(BF16), k_ps[p])
                dct1 = _dot(jnp.where(second_half, vtw, 0.0).astype(BF16), k_ps[p])
                dn = _dot(jnp.where(n_rows, jnp.broadcast_to(w2_r, (16, 128)), 0.0).astype(BF16), k_ps[p])
                sp0 = sp_sb[hh:hh + 1, (2 * p) * 128:(2 * p + 1) * 128]
                sp1 = sp_sb[hh:hh + 1, (2 * p + 1) * 128:(2 * p + 2) * 128]
                i0 = (2 * p) * ML_H + hh
                ct_ref[i0] = sp0 * cts[p][0] + dct0
                ct_ref[i0 + ML_H] = sp1 * cts[p][1] + dct1
                n_ref[p * ML_H + hh] = jnp.where(row16 == 0, sp0, sp1) * n_olds[p] + dn
            yield
        o_ml = (mh_ref[...] * mlng_ref[...] + mlskip_ref[...] * xc) * _silu(z)
        res['ml'] = _dot(o_ml.astype(BF16), wout_ref[HG_W + S5_W:, :])

    _interleave([s5(), mlstm(), hgrn()])

    out = x + (res['hg'] + res['s5'] + res['ml'])
    o_ref[...] = out.reshape(NB, TT, D_MODEL)


def _ffn_kernel(x_ref, g_ref, wg_ref, wu_ref, wd_ref, gfin_ref, o_ref, *, final_norm):
    x = x_ref[...]
    h = (x * _rms_scale(x) * g_ref[...]).astype(BF16)
    acc = x
    for c in range(D_FF // FFN_TF):
        cs = slice(c * FFN_TF, (c + 1) * FFN_TF)
        gate = _dot(h, wg_ref[:, cs])
        up = _dot(h, wu_ref[:, cs])
        act = (_silu(gate) * up).astype(BF16)
        acc = acc + _dot(act, wd_ref[cs, :])
    if final_norm:
        acc = acc * _rms_scale(acc) * gfin_ref[...]
    o_ref[...] = acc


def _np_masks():
    i256 = np.arange(256)
    lc = ((i256[:, None] // HG_L) == (i256[None, :] // HG_L)) & (i256[None, :] <= i256[:, None])
    i128 = np.arange(128)
    hg_causal = ((i128[:, None] // HG_L) == (i128[None, :] // HG_L)) & (i128[None, :] <= i128[:, None])
    hgmask = np.tile(hg_causal, (1, HG_H))
    headblk = (i256[:, None] // HG_DK) == (i256[None, :] // HG_DK)
    mlmask_t = ((i128[:, None] // TT) == (i128[None, :] // TT)) & (i128[:, None] <= i128[None, :])
    selpair = np.zeros((128, 256), np.float32)
    for j in range(2):
        selpair[j * TT + TT - 1, j * 128:(j + 1) * 128] = 1.0
    return dict(lc=lc.astype(np.float32), hgmask=hgmask.astype(np.float32),
                headblk=headblk.astype(np.float32), mlmask_t=mlmask_t.astype(np.float32), selpair=selpair)


_MASKS = _np_masks()


def _const_spec(shape):
    nd = len(shape)
    return pl.BlockSpec(shape, lambda *_: (0,) * nd, pipeline_mode=pl.Buffered(1))


class _Layer:
    def __init__(self, stacked, layer):
        self.stacked, self.layer = stacked, layer


def _operand(c):
    if isinstance(c, _Layer):
        layer = c.layer
        spec = pl.BlockSpec((None,) + c.stacked.shape[1:], lambda *_: (layer, 0, 0),
                            pipeline_mode=pl.Buffered(1))
        return c.stacked, spec
    return c, _const_spec(c.shape)


def _pad_rows(a, rows):
    return jnp.concatenate([a, jnp.zeros((rows - a.shape[0],) + a.shape[1:], a.dtype)], axis=0)


def _s5_params(lam_re, lam_im, log_dt, b_re, b_im, c_re, c_im):
    lre = lam_re.astype(F32)
    lim = lam_im.astype(F32)
    dt = jnp.exp(log_dt.astype(F32))[:, None]
    mag = jnp.exp(lre * dt)
    ang = lim * dt
    ar, ai = mag * jnp.cos(ang), mag * jnp.sin(ang)
    den = lre * lre + lim * lim
    nr, ni = ar - 1.0, ai
    cr = (nr * lre + ni * lim) / den
    ci = (ni * lre - nr * lim) / den
    bbr = cr[..., None] * b_re - ci[..., None] * b_im
    bbi = cr[..., None] * b_im + ci[..., None] * b_re
    eye = jnp.eye(S5_G, dtype=F32)
    blk_r = jnp.einsum('gpc,gk->gckp', bbr, eye).reshape(S5_W, S5_N)
    blk_i = jnp.einsum('gpc,gk->gckp', bbi, eye).reshape(S5_W, S5_N)
    bblk = jnp.concatenate([blk_r, blk_i], axis=1).astype(BF16)
    cb_r = jnp.einsum('gcp,gk->kpgc', c_re, eye).reshape(S5_N, S5_W)
    cb_i = jnp.einsum('gcp,gk->kpgc', c_im, eye).reshape(S5_N, S5_W)
    cblk = jnp.concatenate([cb_r, -cb_i], axis=0).astype(BF16)
    a = _pad_rows(jnp.stack([ar.reshape(S5_N), ai.reshape(S5_N)], axis=0), 8)
    return bblk, cblk, a


def _mix_layer(x, gmix, w_in, lb, hg_ng, s5, s5_d, w_glu, conv_w, conv_b, wq, wk, wv, w_gates, b_gates,
               ml_ng, ml_skip, w_out):
    B, S, _ = x.shape
    assert B % NB == 0 and S % TT == 0
    mk = _MASKS
    hgc = _pad_rows(jnp.stack([jnp.log(lb), jnp.log1p(-lb), 1.0 - lb], axis=0), 8)
    bblk, cblk, s5a = s5
    wg3 = w_gates.reshape(3, ML_H, ML_DH, 2 * ML_H)
    wgx = (jnp.einsum('hde,hec->hdc', wq, wg3[0]) + (ML_DH ** -0.5) * jnp.einsum('hde,hec->hdc', wk, wg3[1]))
    wgm = jnp.einsum('hde,hec->hdc', wv, wg3[2])
    zrow = jnp.zeros((8 - ML_H, ML_W), F32)

    def gate_rows(w):
        wt = w.reshape(ML_W, 2 * ML_H).T
        return jnp.concatenate([wt[:ML_H], zrow, wt[ML_H:], zrow], axis=0).astype(BF16)

    zb = jnp.zeros((8 - ML_H,), F32)
    bg = jnp.broadcast_to(jnp.concatenate([b_gates[:ML_H], zb, b_gates[ML_H:], zb])[:, None], (16, R))
    consts = [
        gmix[None, :], w_in,
        hgc, hg_ng[None, :], jnp.asarray(mk['lc'], BF16), jnp.asarray(mk['hgmask'], F32),
        jnp.asarray(mk['headblk'], BF16),
        bblk, cblk, s5a, s5_d[None, :],
        w_glu.astype(BF16),
        _pad_rows(conv_w, 8), conv_b[None, :], jnp.concatenate([wq, wk], axis=-1).astype(BF16),
        jnp.swapaxes(wv, 1, 2).astype(BF16),
        gate_rows(wgx), gate_rows(wgm), bg,
        jnp.asarray(mk['mlmask_t'], BF16), jnp.asarray(mk['selpair'], BF16), jnp.asarray(mk['mlmask_t'], F32),
        ml_ng[None, :], ml_skip[None, :], w_out,
    ]
    operands = [_operand(c) for c in consts]
    xspec = pl.BlockSpec((NB, TT, D_MODEL), lambda bi, si: (bi, si, 0))
    scratch = [
        pltpu.VMEM((NB, HG_W, HG_W), F32),
        pltpu.VMEM((NB, 2 * S5_N), F32),
        pltpu.VMEM((TT, NB, 2 * S5_N), F32),
        pltpu.VMEM((NB * ML_H, ML_DH, ML_DH), F32),
        pltpu.VMEM((NB // 2 * ML_H, 16, ML_DH), F32),
        pltpu.VMEM((8, NB * 128), F32),
        pltpu.VMEM((NB * 8, ML_W), F32),
        pltpu.VMEM((R, ML_W), F32),
    ]
    return pl.pallas_call(
        _mix_kernel,
        grid=(B // NB, S // TT),
        in_specs=[xspec] + [spec for _, spec in operands],
        out_specs=xspec,
        out_shape=jax.ShapeDtypeStruct(x.shape, F32),
        scratch_shapes=scratch,
        compiler_params=pltpu.CompilerParams(
            dimension_semantics=("parallel", "arbitrary"), vmem_limit_bytes=VMEM_LIMIT),
        name="mixer",
    )(x, *[arr for arr, _ in operands])


def _ffn_layer(x2, g, wg, wu, wd, gfin, final_norm):
    T = x2.shape[0]
    assert T % FFN_TM == 0
    operands = [_operand(c) for c in (g[None, :], wg, wu, wd, gfin[None, :])]
    xspec = pl.BlockSpec((FFN_TM, D_MODEL), lambda i: (i, 0))
    return pl.pallas_call(
        functools.partial(_ffn_kernel, final_norm=final_norm),
        grid=(T // FFN_TM,),
        in_specs=[xspec] + [spec for _, spec in operands],
        out_specs=xspec,
        out_shape=jax.ShapeDtypeStruct(x2.shape, F32),
        compiler_params=pltpu.CompilerParams(
            dimension_semantics=("parallel",), vmem_limit_bytes=VMEM_LIMIT),
        name="ffn",
    )(x2, *[arr for arr, _ in operands])


def kernel(x, norm_mix_g, w_in, hgrn_lb_raw, hgrn_norm_g, s5_lambda_re, s5_lambda_im, s5_log_dt, s5_b_re, s5_b_im, s5_c_re, s5_c_im, s5_d, s5_w_glu, ml_conv_w, ml_conv_b, ml_wq, ml_wk, ml_wv, ml_w_gates, ml_b_gates, ml_norm_g, ml_skip, w_out, norm_ffn_g, ffn_w_gate, ffn_w_up, ffn_w_down, norm_final_g):
    B, S, D = x.shape
    depth = w_in.shape[0]
    lb_cs = jnp.cumsum(jax.nn.softmax(hgrn_lb_raw.astype(F32), axis=0), axis=0)
    lb_all = lb_cs - lb_cs[0]
    w_in_b, w_out_b = w_in.astype(BF16), w_out.astype(BF16)
    wg_b, wu_b, wd_b = ffn_w_gate.astype(BF16), ffn_w_up.astype(BF16), ffn_w_down.astype(BF16)
    for l in range(depth):
        s5 = _s5_params(s5_lambda_re[l], s5_lambda_im[l], s5_log_dt[l], s5_b_re[l], s5_b_im[l],
                        s5_c_re[l], s5_c_im[l])
        x = _mix_layer(x, norm_mix_g[l], _Layer(w_in_b, l), lb_all[l], hgrn_norm_g[l], s5, s5_d[l], s5_w_glu[l],
                       ml_conv_w[l], ml_conv_b[l], ml_wq[l], ml_wk[l], ml_wv[l], ml_w_gates[l],
                       ml_b_gates[l], ml_norm_g[l], ml_skip[l], _Layer(w_out_b, l))
        x2 = _ffn_layer(x.reshape(B * S, D), norm_ffn_g[l], _Layer(wg_b, l), _Layer(wu_b, l), _Layer(wd_b, l),
                        norm_final_g, final_norm=(l == depth - 1))
        x = x2.reshape(B, S, D)
    return x
```

```python
import functools
import math

import numpy as np
import jax
import jax.numpy as jnp
from jax import lax
from jax.experimental import pallas as pl
from jax.experimental.pallas import tpu as pltpu

F32 = jnp.float32
BF16 = jnp.bfloat16

D_MODEL = 1024
HG_W = 256
HG_H = 4
HG_DK = HG_W // HG_H
HG_L = 32
S5_W = 256
S5_G = 16
S5_GC = 16
S5_P = 64
S5_N = S5_G * S5_P
ML_W = 512
ML_H = 4
ML_DH = ML_W // ML_H
CONV_K = 4
D_FF = 2816
D_IN = 4 * HG_W + S5_W + 2 * ML_W
EPS = 1e-6

NB = 8
TT = 64
R = NB * TT
FFN_TM = 1024
FFN_TF = 256
S5_SCAN_STAGE = 32
VMEM_LIMIT = 58 * 1024 * 1024

_O_HQ, _O_HF, _O_HI, _O_HG, _O_SU, _O_MX, _O_MZ = 0, 256, 512, 768, 1024, 1280, 1792


def _dot(a, b):
    return jnp.dot(a, b, preferred_element_type=F32)


def _dot_nt(a, b):
    return lax.dot_general(a, b, (((1,), (1,)), ((), ())), preferred_element_type=F32)


def _dot_tn(a, b):
    return lax.dot_general(a, b, (((0,), (0,)), ((), ())), preferred_element_type=F32)


def _split3(x):
    hi = x.astype(BF16)
    r1 = x - hi.astype(F32)
    mid = r1.astype(BF16)
    lo = (r1 - mid.astype(F32)).astype(BF16)
    return hi, mid, lo


def _sel_dot(m01, x):
    return _dot(jnp.concatenate([m01, m01, m01], axis=1), jnp.concatenate(_split3(x), axis=0))


def _silu(v):
    hv = 0.5 * v
    return hv * jnp.tanh(hv) + hv


def _log_sigmoid(v):
    return jnp.minimum(v, 0.0) - jnp.log1p(jnp.exp(-jnp.abs(v)))


def _rms_scale(v):
    return lax.rsqrt(jnp.mean(v * v, axis=-1, keepdims=True) + EPS)


def _interleave(stage_generators):
    live = list(stage_generators)
    while live:
        for g in list(live):
            try:
                next(g)
            except StopIteration:
                live.remove(g)


def _mix_kernel(x_ref, gmix_ref, win_ref,
                hgc_ref, hgng_ref, lc_ref, hgmask_ref, headblk_ref,
                bblk_ref, cblk_ref, s5a_ref, s5d_ref, wglu_ref,
                convw_ref, convb_ref, wqk_ref, wvt_ref, wgx_ref, wgm_ref, bg_ref,
                l128_ref, selpair_ref, mlmask_t_ref, mlng_ref, mlskip_ref, wout_ref,
                o_ref,
                hst_ref, s5st_ref, bu_ref, ct_ref, n_ref, msb_ref, prev_ref, mh_ref):
    si = pl.program_id(1)

    @pl.when(si == 0)
    def _reset_state():
        hst_ref[...] = jnp.zeros_like(hst_ref)
        s5st_ref[...] = jnp.zeros_like(s5st_ref)
        ct_ref[...] = jnp.zeros_like(ct_ref)
        n_ref[...] = jnp.zeros_like(n_ref)
        msb_ref[...] = jnp.zeros_like(msb_ref)
        prev_ref[...] = jnp.zeros_like(prev_ref)

    x = x_ref[...].reshape(R, D_MODEL)
    h = (x * _rms_scale(x) * gmix_ref[...]).astype(BF16)
    res = {}

    def hgrn():
        zq = _dot(h, win_ref[:, _O_HQ:_O_HQ + HG_W])
        zf = _dot(h, win_ref[:, _O_HF:_O_HF + HG_W])
        vi = _dot(h, win_ref[:, _O_HI:_O_HI + HG_W])
        zg = _dot(h, win_ref[:, _O_HG:_O_HG + HG_W])
        yield
        q = _silu(zq)
        lb = hgc_ref[0:1, :]
        one_mlb = hgc_ref[1:2, :]
        e = jnp.exp(-jnp.abs(zf))
        rcp = 1.0 / (1.0 + e)
        small = e * rcp
        pos = zf >= 0.0
        logf = jnp.log(lb + one_mlb * jnp.where(pos, rcp, small))
        kk = one_mlb * jnp.where(pos, small, rcp)
        lc = lc_ref[...]
        half = R // 2
        b = jnp.concatenate([_sel_dot(lc, logf[:half]), _sel_dot(lc, logf[half:])], axis=0)
        yield
        b3 = b.reshape(R // HG_L, HG_L, HG_W)
        blast = jnp.broadcast_to(b3[:, HG_L - 1:HG_L, :], b3.shape).reshape(R, HG_W)
        qd = (q * jnp.exp(b)).astype(BF16)
        ki = kk * jnp.exp(-b)
        ke = (kk * jnp.exp(blast - b)).astype(BF16)
        dec = jnp.exp(blast)
        vib = vi.astype(BF16)
        lane = lax.broadcasted_iota(jnp.int32, (1, HG_W), 1)
        head_lane = [(lane // HG_DK) == hh for hh in range(HG_H)]
        hgmask = hgmask_ref[...] > 0.5
        GR = 128
        scs = []
        for g in range(R // GR):
            rs = slice(g * GR, (g + 1) * GR)
            ki_g = ki[rs]
            kblk = jnp.concatenate([jnp.where(head_lane[hh], ki_g, 0.0) for hh in range(HG_H)], axis=0)
            scs.append(_dot_nt(qd[rs], kblk.astype(BF16)))
        yield
        oi_parts = []
        for g in range(R // GR):
            rs = slice(g * GR, (g + 1) * GR)
            vi_g = vi[rs]
            sc = jnp.where(hgmask, scs[g], 0.0)
            vstk = jnp.concatenate([jnp.where(head_lane[hh], vi_g, 0.0) for hh in range(HG_H)], axis=0)
            oi_parts.append(_dot(sc.astype(BF16), vstk.astype(BF16)))
        o_intra = jnp.concatenate(oi_parts, axis=0)
        yield
        headblk_bf = headblk_ref[...]
        headblk = headblk_bf.astype(F32) > 0.5
        sts = [hst_ref[bi] for bi in range(NB)]
        hoi = [[None] * (TT // HG_L) for _ in range(NB)]
        for cc in range(TT // HG_L):
            dsts = []
            for bi in range(NB):
                r0 = bi * TT + cc * HG_L
                rows = slice(r0, r0 + HG_L)
                hoi[bi][cc] = _dot_nt(qd[rows], sts[bi].astype(BF16))
                dsts.append(_dot_tn(vib[rows], ke[rows]))
            yield
            for bi in range(NB):
                r0 = bi * TT + cc * HG_L
                sts[bi] = dec[r0:r0 + 1, :] * sts[bi] + jnp.where(headblk, dsts[bi], 0.0)
        for bi in range(NB):
            hst_ref[bi] = sts[bi]
        o = o_intra + jnp.concatenate([part for per_b in hoi for part in per_b], axis=0)
        o2 = o * o
        o2_hi = o2.astype(BF16)
        o2_lo = (o2 - o2_hi.astype(F32)).astype(BF16)
        ms_h = (_dot(o2_hi, headblk_bf) + _dot(o2_lo, headblk_bf)) * (1.0 / HG_DK)
        yield
        o_hg = o * lax.rsqrt(ms_h + EPS) * hgng_ref[...] * _silu(zg)
        res['hg'] = _dot(o_hg.astype(BF16), wout_ref[0:HG_W, :])

    def s5():
        u = _dot(h, win_ref[:, _O_SU:_O_SU + S5_W])
        yield
        u_tb = jnp.swapaxes(u.reshape(NB, TT, S5_W), 0, 1).reshape(R, S5_W).astype(BF16)
        yield
        bu_ref[...] = _dot(u_tb, bblk_ref[...]).reshape(TT, NB, 2 * S5_N)
        yield
        ar = s5a_ref[0:1, :]
        ai = s5a_ref[1:2, :]
        x0 = s5st_ref[...]
        xr, xi = x0[:, :S5_N], x0[:, S5_N:]
        for t in range(TT):
            but = bu_ref[t]
            xr, xi = ar * xr - ai * xi + but[:, :S5_N], ar * xi + ai * xr + but[:, S5_N:]
            bu_ref[t] = jnp.concatenate([xr, xi], axis=-1)
            if t % S5_SCAN_STAGE == S5_SCAN_STAGE - 1:
                yield
        s5st_ref[...] = jnp.concatenate([xr, xi], axis=-1)
        xs = bu_ref[...].reshape(R, 2 * S5_N)
        y_tb = _dot(xs.astype(BF16), cblk_ref[...])
        yield
        y = jnp.swapaxes(y_tb.reshape(TT, NB, S5_W), 0, 1).reshape(R, S5_W) + s5d_ref[...] * u
        yield
        y = 0.5 * y * (1.0 + jnp.tanh(math.sqrt(2.0 / math.pi) * (y + 0.044715 * (y * y * y))))
        glu = _dot(y.astype(BF16), wglu_ref[...])
        yield
        o_s5 = y * (0.5 * jnp.tanh(0.5 * glu) + 0.5)
        res['s5'] = _dot(o_s5.astype(BF16), wout_ref[HG_W:HG_W + S5_W, :])

    def mlstm():
        xm = _dot(h, win_ref[:, _O_MX:_O_MX + ML_W])
        z = _dot(h, win_ref[:, _O_MZ:_O_MZ + ML_W])
        yield
        tail = prev_ref[...]
        xm3 = xm.reshape(NB, TT, ML_W)
        prev_ref[...] = xm3[:, TT - 8:, :].reshape(NB * 8, ML_W)
        t8 = lax.broadcasted_iota(jnp.int32, (1, 8, 1), 1)
        conv = xm * convw_ref[CONV_K - 1:CONV_K, :]
        for k in range(1, CONV_K):
            rolled = pltpu.roll(xm, k, 0).reshape(NB, TT, ML_W)
            tail_k = pltpu.roll(tail, NB * 8 - 8 + k, 0).reshape(NB, 8, ML_W)
            head = jnp.where(t8 >= k, rolled[:, 0:8, :], tail_k)
            shifted = jnp.concatenate([head, rolled[:, 8:, :]], axis=1).reshape(R, ML_W)
            conv = conv + shifted * convw_ref[CONV_K - 1 - k:CONV_K - k, :]
        conv = conv + convb_ref[...]
        xc = _silu(conv)
        xcb = xc.astype(BF16)
        xmb = xm.astype(BF16)
        q_hs, k_hs, vt_hs = [], [], []
        for hh in range(ML_H):
            ls = slice(hh * ML_DH, (hh + 1) * ML_DH)
            qk = _dot(xcb[:, ls], wqk_ref[hh])
            q_hs.append(qk[:, :ML_DH])
            k_hs.append(qk[:, ML_DH:] * (ML_DH ** -0.5))
            vt_hs.append(_dot_nt(wvt_ref[hh], xmb[:, ls]))
        yield
        q_all = jnp.concatenate(q_hs, axis=-1).astype(BF16)
        k_all = jnp.concatenate(k_hs, axis=-1).astype(BF16)
        vt_all = jnp.concatenate(vt_hs, axis=0).astype(BF16)
        g = _dot_nt(wgx_ref[...], xcb) + _dot_nt(wgm_ref[...], xmb) + bg_ref[...]
        yield
        pairs = range(NB // 2)
        tls = [slice(p * 128, (p + 1) * 128) for p in pairs]
        l128 = l128_ref[...]
        lsg3 = jnp.concatenate(_split3(_log_sigmoid(g)), axis=0)
        cs = [_dot(lsg3[:, tl], l128) for tl in tls]
        b16 = jnp.concatenate([c[0:16] + c[16:32] + c[32:48] for c in cs], axis=-1)
        yield
        ig8 = g[0:8]
        b8 = b16[8:16]
        v8 = ig8 - b8
        tlane = lax.broadcasted_iota(jnp.int32, (1, R), 1) % TT
        cm = v8
        for sh in (1, 2, 4, 8, 16, 32):
            cm = jnp.maximum(cm, jnp.where(tlane >= sh, pltpu.roll(cm, sh, 1), -jnp.inf))
        yield
        selpair = selpair_ref[...]
        cb3 = jnp.concatenate(_split3(jnp.concatenate([cm, b8], axis=0)), axis=0)
        ss = [_dot(cb3[:, tl], selpair) for tl in tls]
        sb = jnp.concatenate([c[0:16] + c[16:32] + c[32:48] for c in ss], axis=-1)
        yield
        vmax_sb = sb[0:8]
        glast_sb = sb[8:16]
        m_prev_sb = msb_ref[...]
        m_loc_sb = glast_sb + vmax_sb
        m_new_sb = jnp.maximum(glast_sb + m_prev_sb, m_loc_sb)
        sp_sb = jnp.exp(glast_sb + m_prev_sb - m_new_sb)
        sl_sb = jnp.exp(m_loc_sb - m_new_sb)
        msb_ref[...] = m_new_sb
        lane128 = lax.broadcasted_iota(jnp.int32, (1, 128), 1)
        first_half = lane128 < TT

        def per_time(sbarr):
            parts = [jnp.where(first_half, sbarr[:, (2 * p) * 128:(2 * p + 1) * 128],
                               sbarr[:, (2 * p + 1) * 128:(2 * p + 2) * 128]) for p in range(NB // 2)]
            return jnp.concatenate(parts, axis=-1)

        m_prev_t = per_time(m_prev_sb)
        mbig8 = jnp.maximum(cm, m_prev_t)
        scl8 = jnp.exp(m_prev_t - mbig8)
        en8 = jnp.exp(-(b8 + mbig8))
        w2_8 = per_time(sl_sb) * jnp.exp(v8 - per_time(vmax_sb))

        row16 = lax.broadcasted_iota(jnp.int32, (16, 1), 0)
        v_pieces = [piece.astype(F32) for piece in _split3(v8)]
        m_pieces = [piece.astype(F32) for piece in _split3(mbig8)]
        ones_hi = ((row16 >= 3) & (row16 < 6)).astype(F32)
        ones_lo = (row16 < 3).astype(F32)
        mlmask_t = mlmask_t_ref[...] > 0.5
        second_half = jnp.logical_not(first_half)
        n_rows = ((row16 == 0) & first_half) | ((row16 == 1) & second_half)

        for hh in range(ML_H):
            a_op = ones_hi
            b_op = ones_lo
            for i in range(3):
                a_op = a_op + jnp.where(row16 == i, jnp.broadcast_to(v_pieces[i][hh:hh + 1], (16, R)), 0.0)
                b_op = b_op - jnp.where(row16 == 3 + i, jnp.broadcast_to(m_pieces[i][hh:hh + 1], (16, R)), 0.0)
            a_op = a_op.astype(BF16)
            b_op = b_op.astype(BF16)
            ls = slice(hh * ML_DH, (hh + 1) * ML_DH)
            q_ps = [q_all[tl, ls] for tl in tls]
            k_ps = [k_all[tl, ls] for tl in tls]
            vt_ps = [vt_all[ls, tl] for tl in tls]
            logdms = [_dot_tn(a_op[:, tl], b_op[:, tl]) for tl in tls]
            sts = [_dot_nt(k_ps[p], q_ps[p]) for p in pairs]
            yield
            w_ts, num_ts, inter_ts, qns, cts, n_olds = [], [], [], [], [], []
            for p in pairs:
                w_t = sts[p] * jnp.where(mlmask_t, jnp.exp(logdms[p]), 0.0)
                i0 = (2 * p) * ML_H + hh
                ct0 = ct_ref[i0]
                ct1 = ct_ref[i0 + ML_H]
                n_old = n_ref[p * ML_H + hh]
                w_ts.append(w_t)
                num_ts.append(_dot(vt_ps[p], w_t.astype(BF16)))
                inter_ts.append(jnp.where(first_half, _dot_nt(ct0.astype(BF16), q_ps[p]),
                                          _dot_nt(ct1.astype(BF16), q_ps[p])))
                qns.append(_dot_nt(n_old.astype(BF16), q_ps[p]))
                cts.append((ct0, ct1))
                n_olds.append(n_old)
            yield
            for p in pairs:
                tl = tls[p]
                den_r = jnp.sum(w_ts[p], axis=0, keepdims=True)
                qn_r = jnp.where(first_half, qns[p][0:1], qns[p][1:2])
                scl_r = scl8[hh:hh + 1, tl]
                den = den_r + scl_r * qn_r
                h_t = (num_ts[p] + scl_r * inter_ts[p]) / jnp.maximum(jnp.abs(den), en8[hh:hh + 1, tl])
                hn_t = h_t * lax.rsqrt(jnp.mean(h_t * h_t, axis=0, keepdims=True) + EPS)
                mh_ref[tl, ls] = hn_t.T
                w2_r = w2_8[hh:hh + 1, tl]
                vtw = vt_ps[p].astype(F32) * w2_r
                dct0 = _dot(jnp.where(first_half, vtw, 0.0).astype(BF16), k_ps[p])
                dct1 = _dot(jnp.where(second_half, vtw, 0.0).astype(BF16), k_ps[p])
                dn = _dot(jnp.where(n_rows, jnp.broadcast_to(w2_r, (16, 128)), 0.0).astype(BF16), k_ps[p])
                sp0 = sp_sb[hh:hh + 1, (2 * p) * 128:(2 * p + 1) * 128]
                sp1 = sp_sb[hh:hh + 1, (2 * p + 1) * 128:(2 * p + 2) * 128]
                i0 = (2 * p) * ML_H + hh
                ct_ref[i0] = sp0 * cts[p][0] + dct0
                ct_ref[i0 + ML_H] = sp1 * cts[p][1] + dct1
                n_ref[p * ML_H + hh] = jnp.where(row16 == 0, sp0, sp1) * n_olds[p] + dn
            yield
        o_ml = (mh_ref[...] * mlng_ref[...] + mlskip_ref[...] * xc) * _silu(z)
        res['ml'] = _dot(o_ml.astype(BF16), wout_ref[HG_W + S5_W:, :])

    _interleave([s5(), mlstm(), hgrn()])

    out = x + (res['hg'] + res['s5'] + res['ml'])
    o_ref[...] = out.reshape(NB, TT, D_MODEL)


def _ffn_kernel(x_ref, g_ref, wg_ref, wu_ref, wd_ref, gfin_ref, o_ref, *, final_norm):
    x = x_ref[...]
    h = (x * _rms_scale(x) * g_ref[...]).astype(BF16)
    acc = x
    for c in range(D_FF // FFN_TF):
        cs = slice(c * FFN_TF, (c + 1) * FFN_TF)
        gate = _dot(h, wg_ref[:, cs])
        up = _dot(h, wu_ref[:, cs])
        act = (_silu(gate) * up).astype(BF16)
        acc = acc + _dot(act, wd_ref[cs, :])
    if final_norm:
        acc = acc * _rms_scale(acc) * gfin_ref[...]
    o_ref[...] = acc


def _np_masks():
    i256 = np.arange(256)
    lc = ((i256[:, None] // HG_L) == (i256[None, :] // HG_L)) & (i256[None, :] <= i256[:, None])
    i128 = np.arange(128)
    hg_causal = ((i128[:, None] // HG_L) == (i128[None, :] // HG_L)) & (i128[None, :] <= i128[:, None])
    hgmask = np.tile(hg_causal, (1, HG_H))
    headblk = (i256[:, None] // HG_DK) == (i256[None, :] // HG_DK)
    mlmask_t = ((i128[:, None] // TT) == (i128[None, :] // TT)) & (i128[:, None] <= i128[None, :])
    selpair = np.zeros((128, 256), np.float32)
    for j in range(2):
        selpair[j * TT + TT - 1, j * 128:(j + 1) * 128] = 1.0
    return dict(lc=lc.astype(np.float32), hgmask=hgmask.astype(np.float32),
                headblk=headblk.astype(np.float32), mlmask_t=mlmask_t.astype(np.float32), selpair=selpair)


_MASKS = _np_masks()


def _const_spec(shape):
    nd = len(shape)
    return pl.BlockSpec(shape, lambda *_: (0,) * nd, pipeline_mode=pl.Buffered(1))


class _Layer:
    def __init__(self, stacked, layer):
        self.stacked, self.layer = stacked, layer


def _operand(c):
    if isinstance(c, _Layer):
        layer = c.layer
        zeros = (0,) * (c.stacked.ndim - 1)
        spec = pl.BlockSpec((None,) + c.stacked.shape[1:], lambda *_: (layer,) + zeros,
                            pipeline_mode=pl.Buffered(1))
        return c.stacked, spec
    return c, _const_spec(c.shape)


def _pad_rows(a, rows):
    return jnp.concatenate([a, jnp.zeros((rows - a.shape[0],) + a.shape[1:], a.dtype)], axis=0)


def _s5_params(lam_re, lam_im, log_dt, b_re, b_im, c_re, c_im):
    lre = lam_re.astype(F32)
    lim = lam_im.astype(F32)
    dt = jnp.exp(log_dt.astype(F32))[:, None]
    mag = jnp.exp(lre * dt)
    ang = lim * dt
    ar, ai = mag * jnp.cos(ang), mag * jnp.sin(ang)
    den = lre * lre + lim * lim
    nr, ni = ar - 1.0, ai
    cr = (nr * lre + ni * lim) / den
    ci = (ni * lre - nr * lim) / den
    bbr = cr[..., None] * b_re - ci[..., None] * b_im
    bbi = cr[..., None] * b_im + ci[..., None] * b_re
    eye = jnp.eye(S5_G, dtype=F32)
    blk_r = jnp.einsum('gpc,gk->gckp', bbr, eye).reshape(S5_W, S5_N)
    blk_i = jnp.einsum('gpc,gk->gckp', bbi, eye).reshape(S5_W, S5_N)
    bblk = jnp.concatenate([blk_r, blk_i], axis=1).astype(BF16)
    cb_r = jnp.einsum('gcp,gk->kpgc', c_re, eye).reshape(S5_N, S5_W)
    cb_i = jnp.einsum('gcp,gk->kpgc', c_im, eye).reshape(S5_N, S5_W)
    cblk = jnp.concatenate([cb_r, -cb_i], axis=0).astype(BF16)
    a = _pad_rows(jnp.stack([ar.reshape(S5_N), ai.reshape(S5_N)], axis=0), 8)
    return bblk, cblk, a


def _layer_params(gmix, lb, hg_ng, lam_re, lam_im, log_dt, b_re, b_im, c_re, c_im, s5_d, w_glu, conv_w, conv_b,
                  wq, wk, wv, w_gates, b_gates, ml_ng, ml_skip):
    hgc = _pad_rows(jnp.stack([lb, 1.0 - lb], axis=0), 8)
    bblk, cblk, s5a = _s5_params(lam_re, lam_im, log_dt, b_re, b_im, c_re, c_im)
    wg3 = w_gates.reshape(3, ML_H, ML_DH, 2 * ML_H)
    wgx = (jnp.einsum('hde,hec->hdc', wq, wg3[0]) + (ML_DH ** -0.5) * jnp.einsum('hde,hec->hdc', wk, wg3[1]))
    wgm = jnp.einsum('hde,hec->hdc', wv, wg3[2])
    zrow = jnp.zeros((8 - ML_H, ML_W), F32)

    def gate_rows(w):
        wt = w.reshape(ML_W, 2 * ML_H).T
        return jnp.concatenate([wt[:ML_H], zrow, wt[ML_H:], zrow], axis=0).astype(BF16)

    zb = jnp.zeros((8 - ML_H,), F32)
    bg = jnp.broadcast_to(jnp.concatenate([b_gates[:ML_H], zb, b_gates[ML_H:], zb])[:, None], (16, R))
    return dict(
        gmix=gmix[None, :], hgc=hgc, hg_ng=hg_ng[None, :], bblk=bblk, cblk=cblk, s5a=s5a, s5_d=s5_d[None, :],
        w_glu=w_glu.astype(BF16), conv_w=_pad_rows(conv_w, 8), conv_b=conv_b[None, :],
        wqk=jnp.concatenate([wq, wk], axis=-1).astype(BF16), wvt=jnp.swapaxes(wv, 1, 2).astype(BF16),
        wgx=gate_rows(wgx), wgm=gate_rows(wgm), bg=bg, ml_ng=ml_ng[None, :], ml_skip=ml_skip[None, :])


def _mix_layer(x, layer, lp, w_in, w_out):
    B, S, _ = x.shape
    assert B % NB == 0 and S % TT == 0
    mk = _MASKS
    p = {k: _Layer(v, layer) for k, v in lp.items()}
    consts = [
        p['gmix'], _Layer(w_in, layer),
        p['hgc'], p['hg_ng'], jnp.asarray(mk['lc'], BF16), jnp.asarray(mk['hgmask'], F32),
        jnp.asarray(mk['headblk'], BF16),
        p['bblk'], p['cblk'], p['s5a'], p['s5_d'],
        p['w_glu'],
        p['conv_w'], p['conv_b'], p['wqk'], p['wvt'],
        p['wgx'], p['wgm'], p['bg'],
        jnp.asarray(mk['mlmask_t'], BF16), jnp.asarray(mk['selpair'], BF16), jnp.asarray(mk['mlmask_t'], F32),
        p['ml_ng'], p['ml_skip'], _Layer(w_out, layer),
    ]
    operands = [_operand(c) for c in consts]
    xspec = pl.BlockSpec((NB, TT, D_MODEL), lambda bi, si: (bi, si, 0))
    scratch = [
        pltpu.VMEM((NB, HG_W, HG_W), F32),
        pltpu.VMEM((NB, 2 * S5_N), F32),
        pltpu.VMEM((TT, NB, 2 * S5_N), F32),
        pltpu.VMEM((NB * ML_H, ML_DH, ML_DH), F32),
        pltpu.VMEM((NB // 2 * ML_H, 16, ML_DH), F32),
        pltpu.VMEM((8, NB * 128), F32),
        pltpu.VMEM((NB * 8, ML_W), F32),
        pltpu.VMEM((R, ML_W), F32),
    ]
    return pl.pallas_call(
        _mix_kernel,
        grid=(B // NB, S // TT),
        in_specs=[xspec] + [spec for _, spec in operands],
        out_specs=xspec,
        out_shape=jax.ShapeDtypeStruct(x.shape, F32),
        scratch_shapes=scratch,
        compiler_params=pltpu.CompilerParams(
            dimension_semantics=("parallel", "arbitrary"), vmem_limit_bytes=VMEM_LIMIT),
        name="mixer",
    )(x, *[arr for arr, _ in operands])


def _ffn_layer(x2, g, wg, wu, wd, gfin, final_norm):
    T = x2.shape[0]
    assert T % FFN_TM == 0
    operands = [_operand(c) for c in (g, wg, wu, wd, gfin[None, :])]
    xspec = pl.BlockSpec((FFN_TM, D_MODEL), lambda i: (i, 0))
    return pl.pallas_call(
        functools.partial(_ffn_kernel, final_norm=final_norm),
        grid=(T // FFN_TM,),
        in_specs=[xspec] + [spec for _, spec in operands],
        out_specs=xspec,
        out_shape=jax.ShapeDtypeStruct(x2.shape, F32),
        compiler_params=pltpu.CompilerParams(
            dimension_semantics=("parallel",), vmem_limit_bytes=VMEM_LIMIT),
        name="ffn",
    )(x2, *[arr for arr, _ in operands])


def kernel(x, norm_mix_g, w_in, hgrn_lb_raw, hgrn_norm_g, s5_lambda_re, s5_lambda_im, s5_log_dt, s5_b_re, s5_b_im, s5_c_re, s5_c_im, s5_d, s5_w_glu, ml_conv_w, ml_conv_b, ml_wq, ml_wk, ml_wv, ml_w_gates, ml_b_gates, ml_norm_g, ml_skip, w_out, norm_ffn_g, ffn_w_gate, ffn_w_up, ffn_w_down, norm_final_g):
    B, S, D = x.shape
    depth = w_in.shape[0]
    lb_cs = jnp.cumsum(jax.nn.softmax(hgrn_lb_raw.astype(F32), axis=0), axis=0)
    lb_all = lb_cs - lb_cs[0]
    w_in_b, w_out_b = w_in.astype(BF16), w_out.astype(BF16)
    wg_b, wu_b, wd_b = ffn_w_gate.astype(BF16), ffn_w_up.astype(BF16), ffn_w_down.astype(BF16)
    lp = jax.vmap(_layer_params)(
        norm_mix_g, lb_all, hgrn_norm_g, s5_lambda_re, s5_lambda_im, s5_log_dt, s5_b_re, s5_b_im, s5_c_re,
        s5_c_im, s5_d, s5_w_glu, ml_conv_w, ml_conv_b, ml_wq, ml_wk, ml_wv, ml_w_gates, ml_b_gates,
        ml_norm_g, ml_skip)
    g_ffn = norm_ffn_g[:, None, :]
    for l in range(depth):
        x = _mix_layer(x, l, lp, w_in_b, w_out_b)
        x2 = _ffn_layer(x.reshape(B * S, D), _Layer(g_ffn, l), _Layer(wg_b, l), _Layer(wu_b, l), _Layer(wd_b, l),
                        norm_final_g, final_norm=(l == depth - 1))
        x = x2.reshape(B, S, D)
    return x
```

```python
import functools
import math

import numpy as np
import jax
import jax.numpy as jnp
from jax import lax
from jax.experimental import pallas as pl
from jax.experimental.pallas import tpu as pltpu

F32 = jnp.float32
BF16 = jnp.bfloat16

D_MODEL = 1024
HG_W = 256
HG_H = 4
HG_DK = HG_W // HG_H
HG_L = 32
S5_W = 256
S5_G = 16
S5_GC = 16
S5_P = 64
S5_N = S5_G * S5_P
ML_W = 512
ML_H = 4
ML_DH = ML_W // ML_H
CONV_K = 4
D_FF = 2816
D_IN = 4 * HG_W + S5_W + 2 * ML_W
EPS = 1e-6

NB = 16
HG_CUM = 256
TT = 64
R = NB * TT
FFN_TM = 1024
FFN_TF = 256
S5_SCAN_STAGE = 32
VMEM_LIMIT = 62 * 1024 * 1024

_O_HQ, _O_HF, _O_HI, _O_HG, _O_SU, _O_MX, _O_MZ = 0, 256, 512, 768, 1024, 1280, 1792


def _dot(a, b):
    return jnp.dot(a, b, preferred_element_type=F32)


def _dot_nt(a, b):
    return lax.dot_general(a, b, (((1,), (1,)), ((), ())), preferred_element_type=F32)


def _dot_tn(a, b):
    return lax.dot_general(a, b, (((0,), (0,)), ((), ())), preferred_element_type=F32)


def _split3(x):
    hi = x.astype(BF16)
    r1 = x - hi.astype(F32)
    mid = r1.astype(BF16)
    lo = (r1 - mid.astype(F32)).astype(BF16)
    return hi, mid, lo


def _sel_dot(m01, x):
    return _dot(jnp.concatenate([m01, m01, m01], axis=1), jnp.concatenate(_split3(x), axis=0))


def _silu(v):
    hv = 0.5 * v
    return hv * jnp.tanh(hv) + hv


def _log_sigmoid(v):
    return jnp.minimum(v, 0.0) - jnp.log1p(jnp.exp(-jnp.abs(v)))


def _rms_scale(v):
    return lax.rsqrt(jnp.mean(v * v, axis=-1, keepdims=True) + EPS)


def _interleave(stage_generators):
    live = list(stage_generators)
    while live:
        for g in list(live):
            try:
                next(g)
            except StopIteration:
                live.remove(g)


def _mix_kernel(x_ref, gmix_ref, win_ref,
                hgc_ref, hgng_ref, lc_ref, hgmask_ref, headblk_ref,
                bblk_ref, cblk_ref, s5a_ref, s5d_ref, wglu_ref,
                convw_ref, convb_ref, wqk_ref, wvt_ref, wgx_ref, wgm_ref, bg_ref,
                l128_ref, selpair_ref, mlmask_t_ref, mlng_ref, mlskip_ref, wout_ref,
                o_ref,
                hst_ref, s5st_ref, bu_ref, ct_ref, n_ref, msb_ref, prev_ref, mh_ref):
    si = pl.program_id(1)

    @pl.when(si == 0)
    def _reset_state():
        hst_ref[...] = jnp.zeros_like(hst_ref)
        s5st_ref[...] = jnp.zeros_like(s5st_ref)
        ct_ref[...] = jnp.zeros_like(ct_ref)
        n_ref[...] = jnp.zeros_like(n_ref)
        msb_ref[...] = jnp.zeros_like(msb_ref)
        prev_ref[...] = jnp.zeros_like(prev_ref)

    x = x_ref[...].reshape(R, D_MODEL)
    h = (x * _rms_scale(x) * gmix_ref[...]).astype(BF16)
    res = {}

    def hgrn():
        zq = _dot(h, win_ref[:, _O_HQ:_O_HQ + HG_W])
        zf = _dot(h, win_ref[:, _O_HF:_O_HF + HG_W])
        vi = _dot(h, win_ref[:, _O_HI:_O_HI + HG_W])
        zg = _dot(h, win_ref[:, _O_HG:_O_HG + HG_W])
        yield
        q = _silu(zq)
        lb = hgc_ref[0:1, :]
        one_mlb = hgc_ref[1:2, :]
        e = jnp.exp(-jnp.abs(zf))
        rcp = 1.0 / (1.0 + e)
        small = e * rcp
        pos = zf >= 0.0
        logf = jnp.log(lb + one_mlb * jnp.where(pos, rcp, small))
        kk = one_mlb * jnp.where(pos, small, rcp)
        lc = lc_ref[...]
        b = jnp.concatenate([_sel_dot(lc, logf[i * HG_CUM:(i + 1) * HG_CUM]) for i in range(R // HG_CUM)],
                            axis=0)
        yield
        b3 = b.reshape(R // HG_L, HG_L, HG_W)
        blast = jnp.broadcast_to(b3[:, HG_L - 1:HG_L, :], b3.shape).reshape(R, HG_W)
        qd = (q * jnp.exp(b)).astype(BF16)
        ki = kk * jnp.exp(-b)
        ke = (kk * jnp.exp(blast - b)).astype(BF16)
        dec = jnp.exp(blast)
        vib = vi.astype(BF16)
        lane = lax.broadcasted_iota(jnp.int32, (1, HG_W), 1)
        head_lane = [(lane // HG_DK) == hh for hh in range(HG_H)]
        hgmask = hgmask_ref[...] > 0.5
        GR = 128
        scs = []
        for g in range(R // GR):
            rs = slice(g * GR, (g + 1) * GR)
            ki_g = ki[rs]
            kblk = jnp.concatenate([jnp.where(head_lane[hh], ki_g, 0.0) for hh in range(HG_H)], axis=0)
            scs.append(_dot_nt(qd[rs], kblk.astype(BF16)))
        yield
        oi_parts = []
        for g in range(R // GR):
            rs = slice(g * GR, (g + 1) * GR)
            vi_g = vi[rs]
            sc = jnp.where(hgmask, scs[g], 0.0)
            vstk = jnp.concatenate([jnp.where(head_lane[hh], vi_g, 0.0) for hh in range(HG_H)], axis=0)
            oi_parts.append(_dot(sc.astype(BF16), vstk.astype(BF16)))
        o_intra = jnp.concatenate(oi_parts, axis=0)
        yield
        headblk_bf = headblk_ref[...]
        headblk = headblk_bf.astype(F32) > 0.5
        sts = [hst_ref[bi] for bi in range(NB)]
        hoi = [[None] * (TT // HG_L) for _ in range(NB)]
        for cc in range(TT // HG_L):
            dsts = []
            for bi in range(NB):
                r0 = bi * TT + cc * HG_L
                rows = slice(r0, r0 + HG_L)
                hoi[bi][cc] = _dot_nt(qd[rows], sts[bi].astype(BF16))
                dsts.append(_dot_tn(vib[rows], ke[rows]))
            yield
            for bi in range(NB):
                r0 = bi * TT + cc * HG_L
                sts[bi] = dec[r0:r0 + 1, :] * sts[bi] + jnp.where(headblk, dsts[bi], 0.0)
        for bi in range(NB):
            hst_ref[bi] = sts[bi]
        o = o_intra + jnp.concatenate([part for per_b in hoi for part in per_b], axis=0)
        o2 = o * o
        o2_hi = o2.astype(BF16)
        o2_lo = (o2 - o2_hi.astype(F32)).astype(BF16)
        ms_h = (_dot(o2_hi, headblk_bf) + _dot(o2_lo, headblk_bf)) * (1.0 / HG_DK)
        yield
        o_hg = o * lax.rsqrt(ms_h + EPS) * hgng_ref[...] * _silu(zg)
        res['hg'] = _dot(o_hg.astype(BF16), wout_ref[0:HG_W, :])

    def s5():
        u = _dot(h, win_ref[:, _O_SU:_O_SU + S5_W])
        yield
        u_tb = jnp.swapaxes(u.reshape(NB, TT, S5_W), 0, 1).reshape(R, S5_W).astype(BF16)
        yield
        bu_ref[...] = _dot(u_tb, bblk_ref[...]).reshape(TT, NB, 2 * S5_N)
        yield
        ar = s5a_ref[0:1, :]
        ai = s5a_ref[1:2, :]
        x0 = s5st_ref[...]
        xr, xi = x0[:, :S5_N], x0[:, S5_N:]
        for t in range(TT):
            but = bu_ref[t]
            xr, xi = ar * xr - ai * xi + but[:, :S5_N], ar * xi + ai * xr + but[:, S5_N:]
            bu_ref[t] = jnp.concatenate([xr, xi], axis=-1)
            if t % S5_SCAN_STAGE == S5_SCAN_STAGE - 1:
                yield
        s5st_ref[...] = jnp.concatenate([xr, xi], axis=-1)
        xs = bu_ref[...].reshape(R, 2 * S5_N)
        y_tb = _dot(xs.astype(BF16), cblk_ref[...])
        yield
        y = jnp.swapaxes(y_tb.reshape(TT, NB, S5_W), 0, 1).reshape(R, S5_W) + s5d_ref[...] * u
        yield
        y = 0.5 * y * (1.0 + jnp.tanh(math.sqrt(2.0 / math.pi) * (y + 0.044715 * (y * y * y))))
        glu = _dot(y.astype(BF16), wglu_ref[...])
        yield
        o_s5 = y * (0.5 * jnp.tanh(0.5 * glu) + 0.5)
        res['s5'] = _dot(o_s5.astype(BF16), wout_ref[HG_W:HG_W + S5_W, :])

    def mlstm():
        xm = _dot(h, win_ref[:, _O_MX:_O_MX + ML_W])
        z = _dot(h, win_ref[:, _O_MZ:_O_MZ + ML_W])
        yield
        tail = prev_ref[...]
        xm3 = xm.reshape(NB, TT, ML_W)
        prev_ref[...] = xm3[:, TT - 8:, :].reshape(NB * 8, ML_W)
        t8 = lax.broadcasted_iota(jnp.int32, (1, 8, 1), 1)
        conv = xm * convw_ref[CONV_K - 1:CONV_K, :]
        for k in range(1, CONV_K):
            rolled = pltpu.roll(xm, k, 0).reshape(NB, TT, ML_W)
            tail_k = pltpu.roll(tail, NB * 8 - 8 + k, 0).reshape(NB, 8, ML_W)
            head = jnp.where(t8 >= k, rolled[:, 0:8, :], tail_k)
            shifted = jnp.concatenate([head, rolled[:, 8:, :]], axis=1).reshape(R, ML_W)
            conv = conv + shifted * convw_ref[CONV_K - 1 - k:CONV_K - k, :]
        conv = conv + convb_ref[...]
        xc = _silu(conv)
        xcb = xc.astype(BF16)
        xmb = xm.astype(BF16)
        q_hs, k_hs, vt_hs = [], [], []
        for hh in range(ML_H):
            ls = slice(hh * ML_DH, (hh + 1) * ML_DH)
            qk = _dot(xcb[:, ls], wqk_ref[hh])
            q_hs.append(qk[:, :ML_DH])
            k_hs.append(qk[:, ML_DH:] * (ML_DH ** -0.5))
            vt_hs.append(_dot_nt(wvt_ref[hh], xmb[:, ls]))
        yield
        q_all = jnp.concatenate(q_hs, axis=-1).astype(BF16)
        k_all = jnp.concatenate(k_hs, axis=-1).astype(BF16)
        vt_all = jnp.concatenate(vt_hs, axis=0).astype(BF16)
        g = _dot_nt(wgx_ref[...], xcb) + _dot_nt(wgm_ref[...], xmb) + bg_ref[...]
        yield
        pairs = range(NB // 2)
        tls = [slice(p * 128, (p + 1) * 128) for p in pairs]
        l128 = l128_ref[...]
        lsg3 = jnp.concatenate(_split3(_log_sigmoid(g)), axis=0)
        cs = [_dot(lsg3[:, tl], l128) for tl in tls]
        b16 = jnp.concatenate([c[0:16] + c[16:32] + c[32:48] for c in cs], axis=-1)
        yield
        ig8 = g[0:8]
        b8 = b16[8:16]
        v8 = ig8 - b8
        tlane = lax.broadcasted_iota(jnp.int32, (1, R), 1) % TT
        cm = v8
        for sh in (1, 2, 4, 8, 16, 32):
            cm = jnp.maximum(cm, jnp.where(tlane >= sh, pltpu.roll(cm, sh, 1), -jnp.inf))
        yield
        selpair = selpair_ref[...]
        cb3 = jnp.concatenate(_split3(jnp.concatenate([cm, b8], axis=0)), axis=0)
        ss = [_dot(cb3[:, tl], selpair) for tl in tls]
        sb = jnp.concatenate([c[0:16] + c[16:32] + c[32:48] for c in ss], axis=-1)
        yield
        vmax_sb = sb[0:8]
        glast_sb = sb[8:16]
        m_prev_sb = msb_ref[...]
        m_loc_sb = glast_sb + vmax_sb
        m_new_sb = jnp.maximum(glast_sb + m_prev_sb, m_loc_sb)
        sp_sb = jnp.exp(glast_sb + m_prev_sb - m_new_sb)
        sl_sb = jnp.exp(m_loc_sb - m_new_sb)
        msb_ref[...] = m_new_sb
        lane128 = lax.broadcasted_iota(jnp.int32, (1, 128), 1)
        first_half = lane128 < TT

        def per_time(sbarr):
            parts = [jnp.where(first_half, sbarr[:, (2 * p) * 128:(2 * p + 1) * 128],
                               sbarr[:, (2 * p + 1) * 128:(2 * p + 2) * 128]) for p in range(NB // 2)]
            return jnp.concatenate(parts, axis=-1)

        m_prev_t = per_time(m_prev_sb)
        mbig8 = jnp.maximum(cm, m_prev_t)
        scl8 = jnp.exp(m_prev_t - mbig8)
        en8 = jnp.exp(-(b8 + mbig8))
        w2_8 = per_time(sl_sb) * jnp.exp(v8 - per_time(vmax_sb))

        row16 = lax.broadcasted_iota(jnp.int32, (16, 1), 0)
        v_pieces = [piece.astype(F32) for piece in _split3(v8)]
        m_pieces = [piece.astype(F32) for piece in _split3(mbig8)]
        ones_hi = ((row16 >= 3) & (row16 < 6)).astype(F32)
        ones_lo = (row16 < 3).astype(F32)
        mlmask_t = mlmask_t_ref[...] > 0.5
        second_half = jnp.logical_not(first_half)
        n_rows = ((row16 == 0) & first_half) | ((row16 == 1) & second_half)

        for hh in range(ML_H):
            a_op = ones_hi
            b_op = ones_lo
            for i in range(3):
                a_op = a_op + jnp.where(row16 == i, jnp.broadcast_to(v_pieces[i][hh:hh + 1], (16, R)), 0.0)
                b_op = b_op - jnp.where(row16 == 3 + i, jnp.broadcast_to(m_pieces[i][hh:hh + 1], (16, R)), 0.0)
            a_op = a_op.astype(BF16)
            b_op = b_op.astype(BF16)
            ls = slice(hh * ML_DH, (hh + 1) * ML_DH)
            q_ps = [q_all[tl, ls] for tl in tls]
            k_ps = [k_all[tl, ls] for tl in tls]
            vt_ps = [vt_all[ls, tl] for tl in tls]
            logdms = [_dot_tn(a_op[:, tl], b_op[:, tl]) for tl in tls]
            sts = [_dot_nt(k_ps[p], q_ps[p]) for p in pairs]
            yield
            w_ts, num_ts, inter_ts, qns, cts, n_olds = [], [], [], [], [], []
            for p in pairs:
                w_t = sts[p] * jnp.where(mlmask_t, jnp.exp(logdms[p]), 0.0)
                i0 = (2 * p) * ML_H + hh
                ct0 = ct_ref[i0]
                ct1 = ct_ref[i0 + ML_H]
                n_old = n_ref[p * ML_H + hh]
                w_ts.append(w_t)
                num_ts.append(_dot(vt_ps[p], w_t.astype(BF16)))
                inter_ts.append(jnp.where(first_half, _dot_nt(ct0.astype(BF16), q_ps[p]),
                                          _dot_nt(ct1.astype(BF16), q_ps[p])))
                qns.append(_dot_nt(n_old.astype(BF16), q_ps[p]))
                cts.append((ct0, ct1))
                n_olds.append(n_old)
            yield
            for p in pairs:
                tl = tls[p]
                den_r = jnp.sum(w_ts[p], axis=0, keepdims=True)
                qn_r = jnp.where(first_half, qns[p][0:1], qns[p][1:2])
                scl_r = scl8[hh:hh + 1, tl]
                den = den_r + scl_r * qn_r
                inv_r = 1.0 / jnp.maximum(jnp.abs(den), en8[hh:hh + 1, tl])
                h_t = (num_ts[p] + scl_r * inter_ts[p]) * inv_r
                hn_t = h_t * lax.rsqrt(jnp.mean(h_t * h_t, axis=0, keepdims=True) + EPS)
                mh_ref[tl, ls] = hn_t.T
                w2_r = w2_8[hh:hh + 1, tl]
                vtw = vt_ps[p].astype(F32) * w2_r
                dct0 = _dot(jnp.where(first_half, vtw, 0.0).astype(BF16), k_ps[p])
                dct1 = _dot(jnp.where(second_half, vtw, 0.0).astype(BF16), k_ps[p])
                dn = _dot(jnp.where(n_rows, jnp.broadcast_to(w2_r, (16, 128)), 0.0).astype(BF16), k_ps[p])
                sp0 = sp_sb[hh:hh + 1, (2 * p) * 128:(2 * p + 1) * 128]
                sp1 = sp_sb[hh:hh + 1, (2 * p + 1) * 128:(2 * p + 2) * 128]
                i0 = (2 * p) * ML_H + hh
                ct_ref[i0] = sp0 * cts[p][0] + dct0
                ct_ref[i0 + ML_H] = sp1 * cts[p][1] + dct1
                n_ref[p * ML_H + hh] = jnp.where(row16 == 0, sp0, sp1) * n_olds[p] + dn
            yield
        o_ml = (mh_ref[...] * mlng_ref[...] + mlskip_ref[...] * xc) * _silu(z)
        res['ml'] = _dot(o_ml.astype(BF16), wout_ref[HG_W + S5_W:, :])

    _interleave([s5(), mlstm(), hgrn()])

    out = x + (res['hg'] + res['s5'] + res['ml'])
    o_ref[...] = out.reshape(NB, TT, D_MODEL)


def _ffn_kernel(x_ref, g_ref, wg_ref, wu_ref, wd_ref, gfin_ref, o_ref, *, final_norm):
    x = x_ref[...]
    h = (x * _rms_scale(x) * g_ref[...]).astype(BF16)
    acc = x
    for c in range(D_FF // FFN_TF):
        cs = slice(c * FFN_TF, (c + 1) * FFN_TF)
        gate = _dot(h, wg_ref[:, cs])
        up = _dot(h, wu_ref[:, cs])
        act = (_silu(gate) * up).astype(BF16)
        acc = acc + _dot(act, wd_ref[cs, :])
    if final_norm:
        acc = acc * _rms_scale(acc) * gfin_ref[...]
    o_ref[...] = acc


def _np_masks():
    icum = np.arange(HG_CUM)
    lc = ((icum[:, None] // HG_L) == (icum[None, :] // HG_L)) & (icum[None, :] <= icum[:, None])
    i256 = np.arange(HG_W)
    i128 = np.arange(128)
    hg_causal = ((i128[:, None] // HG_L) == (i128[None, :] // HG_L)) & (i128[None, :] <= i128[:, None])
    hgmask = np.tile(hg_causal, (1, HG_H))
    headblk = (i256[:, None] // HG_DK) == (i256[None, :] // HG_DK)
    mlmask_t = ((i128[:, None] // TT) == (i128[None, :] // TT)) & (i128[:, None] <= i128[None, :])
    selpair = np.zeros((128, 256), np.float32)
    for j in range(2):
        selpair[j * TT + TT - 1, j * 128:(j + 1) * 128] = 1.0
    return dict(lc=lc.astype(np.float32), hgmask=hgmask.astype(np.float32),
                headblk=headblk.astype(np.float32), mlmask_t=mlmask_t.astype(np.float32), selpair=selpair)


_MASKS = _np_masks()


def _const_spec(shape):
    nd = len(shape)
    return pl.BlockSpec(shape, lambda *_: (0,) * nd, pipeline_mode=pl.Buffered(1))


class _Layer:
    def __init__(self, stacked, layer):
        self.stacked, self.layer = stacked, layer


def _operand(c):
    if isinstance(c, _Layer):
        layer = c.layer
        zeros = (0,) * (c.stacked.ndim - 1)
        spec = pl.BlockSpec((None,) + c.stacked.shape[1:], lambda *_: (layer,) + zeros,
                            pipeline_mode=pl.Buffered(1))
        return c.stacked, spec
    return c, _const_spec(c.shape)


def _pad_rows(a, rows):
    return jnp.concatenate([a, jnp.zeros((rows - a.shape[0],) + a.shape[1:], a.dtype)], axis=0)


def _s5_params(lam_re, lam_im, log_dt, b_re, b_im, c_re, c_im):
    lre = lam_re.astype(F32)
    lim = lam_im.astype(F32)
    dt = jnp.exp(log_dt.astype(F32))[:, None]
    mag = jnp.exp(lre * dt)
    ang = lim * dt
    ar, ai = mag * jnp.cos(ang), mag * jnp.sin(ang)
    den = lre * lre + lim * lim
    nr, ni = ar - 1.0, ai
    cr = (nr * lre + ni * lim) / den
    ci = (ni * lre - nr * lim) / den
    bbr = cr[..., None] * b_re - ci[..., None] * b_im
    bbi = cr[..., None] * b_im + ci[..., None] * b_re
    eye = jnp.eye(S5_G, dtype=F32)
    blk_r = jnp.einsum('gpc,gk->gckp', bbr, eye).reshape(S5_W, S5_N)
    blk_i = jnp.einsum('gpc,gk->gckp', bbi, eye).reshape(S5_W, S5_N)
    bblk = jnp.concatenate([blk_r, blk_i], axis=1).astype(BF16)
    cb_r = jnp.einsum('gcp,gk->kpgc', c_re, eye).reshape(S5_N, S5_W)
    cb_i = jnp.einsum('gcp,gk->kpgc', c_im, eye).reshape(S5_N, S5_W)
    cblk = jnp.concatenate([cb_r, -cb_i], axis=0).astype(BF16)
    a = _pad_rows(jnp.stack([ar.reshape(S5_N), ai.reshape(S5_N)], axis=0), 8)
    return bblk, cblk, a


def _layer_params(gmix, lb, hg_ng, lam_re, lam_im, log_dt, b_re, b_im, c_re, c_im, s5_d, w_glu, conv_w, conv_b,
                  wq, wk, wv, w_gates, b_gates, ml_ng, ml_skip):
    hgc = _pad_rows(jnp.stack([lb, 1.0 - lb], axis=0), 8)
    bblk, cblk, s5a = _s5_params(lam_re, lam_im, log_dt, b_re, b_im, c_re, c_im)
    wg3 = w_gates.reshape(3, ML_H, ML_DH, 2 * ML_H)
    wgx = (jnp.einsum('hde,hec->hdc', wq, wg3[0]) + (ML_DH ** -0.5) * jnp.einsum('hde,hec->hdc', wk, wg3[1]))
    wgm = jnp.einsum('hde,hec->hdc', wv, wg3[2])
    zrow = jnp.zeros((8 - ML_H, ML_W), F32)

    def gate_rows(w):
        wt = w.reshape(ML_W, 2 * ML_H).T
        return jnp.concatenate([wt[:ML_H], zrow, wt[ML_H:], zrow], axis=0).astype(BF16)

    zb = jnp.zeros((8 - ML_H,), F32)
    bg = jnp.broadcast_to(jnp.concatenate([b_gates[:ML_H], zb, b_gates[ML_H:], zb])[:, None], (16, R))
    return dict(
        gmix=gmix[None, :], hgc=hgc, hg_ng=hg_ng[None, :], bblk=bblk, cblk=cblk, s5a=s5a, s5_d=s5_d[None, :],
        w_glu=w_glu.astype(BF16), conv_w=_pad_rows(conv_w, 8), conv_b=conv_b[None, :],
        wqk=jnp.concatenate([wq, wk], axis=-1).astype(BF16), wvt=jnp.swapaxes(wv, 1, 2).astype(BF16),
        wgx=gate_rows(wgx), wgm=gate_rows(wgm), bg=bg, ml_ng=ml_ng[None, :], ml_skip=ml_skip[None, :])


def _mix_layer(x, layer, lp, w_in, w_out):
    B, S, _ = x.shape
    assert B % NB == 0 and S % TT == 0
    mk = _MASKS
    p = {k: _Layer(v, layer) for k, v in lp.items()}
    consts = [
        p['gmix'], _Layer(w_in, layer),
        p['hgc'], p['hg_ng'], jnp.asarray(mk['lc'], BF16), jnp.asarray(mk['hgmask'], F32),
        jnp.asarray(mk['headblk'], BF16),
        p['bblk'], p['cblk'], p['s5a'], p['s5_d'],
        p['w_glu'],
        p['conv_w'], p['conv_b'], p['wqk'], p['wvt'],
        p['wgx'], p['wgm'], p['bg'],
        jnp.asarray(mk['mlmask_t'], BF16), jnp.asarray(mk['selpair'], BF16), jnp.asarray(mk['mlmask_t'], F32),
        p['ml_ng'], p['ml_skip'], _Layer(w_out, layer),
    ]
    operands = [_operand(c) for c in consts]
    xspec = pl.BlockSpec((NB, TT, D_MODEL), lambda bi, si: (bi, si, 0))
    scratch = [
        pltpu.VMEM((NB, HG_W, HG_W), F32),
        pltpu.VMEM((NB, 2 * S5_N), F32),
        pltpu.VMEM((TT, NB, 2 * S5_N), F32),
        pltpu.VMEM((NB * ML_H, ML_DH, ML_DH), F32),
        pltpu.VMEM((NB // 2 * ML_H, 16, ML_DH), F32),
        pltpu.VMEM((8, NB * 128), F32),
        pltpu.VMEM((NB * 8, ML_W), F32),
        pltpu.VMEM((R, ML_W), F32),
    ]
    return pl.pallas_call(
        _mix_kernel,
        grid=(B // NB, S // TT),
        in_specs=[xspec] + [spec for _, spec in operands],
        out_specs=xspec,
        out_shape=jax.ShapeDtypeStruct(x.shape, F32),
        scratch_shapes=scratch,
        compiler_params=pltpu.CompilerParams(
            dimension_semantics=("parallel", "arbitrary"), vmem_limit_bytes=VMEM_LIMIT),
        name="mixer",
    )(x, *[arr for arr, _ in operands])


def _ffn_layer(x2, g, wg, wu, wd, gfin, final_norm):
    T = x2.shape[0]
    assert T % FFN_TM == 0
    operands = [_operand(c) for c in (g, wg, wu, wd, gfin[None, :])]
    xspec = pl.BlockSpec((FFN_TM, D_MODEL), lambda i: (i, 0))
    return pl.pallas_call(
        functools.partial(_ffn_kernel, final_norm=final_norm),
        grid=(T // FFN_TM,),
        in_specs=[xspec] + [spec for _, spec in operands],
        out_specs=xspec,
        out_shape=jax.ShapeDtypeStruct(x2.shape, F32),
        compiler_params=pltpu.CompilerParams(
            dimension_semantics=("parallel",), vmem_limit_bytes=VMEM_LIMIT),
        name="ffn",
    )(x2, *[arr for arr, _ in operands])


def kernel(x, norm_mix_g, w_in, hgrn_lb_raw, hgrn_norm_g, s5_lambda_re, s5_lambda_im, s5_log_dt, s5_b_re, s5_b_im, s5_c_re, s5_c_im, s5_d, s5_w_glu, ml_conv_w, ml_conv_b, ml_wq, ml_wk, ml_wv, ml_w_gates, ml_b_gates, ml_norm_g, ml_skip, w_out, norm_ffn_g, ffn_w_gate, ffn_w_up, ffn_w_down, norm_final_g):
    B, S, D = x.shape
    depth = w_in.shape[0]
    lb_cs = jnp.cumsum(jax.nn.softmax(hgrn_lb_raw.astype(F32), axis=0), axis=0)
    lb_all = lb_cs - lb_cs[0]
    w_in_b, w_out_b = w_in.astype(BF16), w_out.astype(BF16)
    wg_b, wu_b, wd_b = ffn_w_gate.astype(BF16), ffn_w_up.astype(BF16), ffn_w_down.astype(BF16)
    lp = jax.vmap(_layer_params)(
        norm_mix_g, lb_all, hgrn_norm_g, s5_lambda_re, s5_lambda_im, s5_log_dt, s5_b_re, s5_b_im, s5_c_re,
        s5_c_im, s5_d, s5_w_glu, ml_conv_w, ml_conv_b, ml_wq, ml_wk, ml_wv, ml_w_gates, ml_b_gates,
        ml_norm_g, ml_skip)
    g_ffn = norm_ffn_g[:, None, :]
    for l in range(depth):
        x = _mix_layer(x, l, lp, w_in_b, w_out_b)
        x2 = _ffn_layer(x.reshape(B * S, D), _Layer(g_ffn, l), _Layer(wg_b, l), _Layer(wu_b, l), _Layer(wd_b, l),
                        norm_final_g, final_norm=(l == depth - 1))
        x = x2.reshape(B, S, D)
    return x
```

```python
import functools
import math

import numpy as np
import jax
import jax.numpy as jnp
from jax import lax
from jax.experimental import pallas as pl
from jax.experimental.pallas import tpu as pltpu

F32 = jnp.float32
BF16 = jnp.bfloat16

D_MODEL = 1024
HG_W = 256
HG_H = 4
HG_DK = HG_W // HG_H
HG_L = 32
S5_W = 256
S5_G = 16
S5_GC = 16
S5_P = 64
S5_N = S5_G * S5_P
ML_W = 512
ML_H = 4
ML_DH = ML_W // ML_H
CONV_K = 4
D_FF = 2816
D_IN = 4 * HG_W + S5_W + 2 * ML_W
EPS = 1e-6

NB = 16
HG_CUM = 256
TT = 64
R = NB * TT
FFN_TM = 1024
FFN_TF = 256
S5_SCAN_STAGE = 32
VMEM_LIMIT = 62 * 1024 * 1024

_O_HQ, _O_HF, _O_HI, _O_HG, _O_SU, _O_MX, _O_MZ = 0, 256, 512, 768, 1024, 1280, 1792


def _dot(a, b):
    return jnp.dot(a, b, preferred_element_type=F32)


def _dot_nt(a, b):
    return lax.dot_general(a, b, (((1,), (1,)), ((), ())), preferred_element_type=F32)


def _dot_tn(a, b):
    return lax.dot_general(a, b, (((0,), (0,)), ((), ())), preferred_element_type=F32)


def _split3(x):
    hi = x.astype(BF16)
    r1 = x - hi.astype(F32)
    mid = r1.astype(BF16)
    lo = (r1 - mid.astype(F32)).astype(BF16)
    return hi, mid, lo


def _sel_dot(m01, x):
    return _dot(jnp.concatenate([m01, m01, m01], axis=1), jnp.concatenate(_split3(x), axis=0))


def _silu(v):
    hv = 0.5 * v
    return hv * jnp.tanh(hv) + hv


def _log_sigmoid(v):
    return jnp.minimum(v, 0.0) - jnp.log1p(jnp.exp(-jnp.abs(v)))


def _rms_scale(v):
    return lax.rsqrt(jnp.mean(v * v, axis=-1, keepdims=True) + EPS)


def _interleave(stage_generators):
    live = list(stage_generators)
    while live:
        for g in list(live):
            try:
                next(g)
            except StopIteration:
                live.remove(g)


def _mix_kernel(x_ref, gmix_ref, win_ref,
                hgc_ref, hgng_ref, lc_ref, hgmask_ref, headblk_ref,
                bblk_ref, cblk_ref, s5a_ref, s5d_ref, wglu_ref,
                convw_ref, convb_ref, wqk_ref, wvt_ref, wgx_ref, wgm_ref, bg_ref,
                l128_ref, selpair_ref, mlmask_t_ref, mlng_ref, mlskip_ref, wout_ref,
                o_ref,
                hst_ref, s5st_ref, bu_ref, ct_ref, n_ref, msb_ref, prev_ref, mh_ref):
    si = pl.program_id(1)

    @pl.when(si == 0)
    def _reset_state():
        hst_ref[...] = jnp.zeros_like(hst_ref)
        s5st_ref[...] = jnp.zeros_like(s5st_ref)
        ct_ref[...] = jnp.zeros_like(ct_ref)
        n_ref[...] = jnp.zeros_like(n_ref)
        msb_ref[...] = jnp.zeros_like(msb_ref)
        prev_ref[...] = jnp.zeros_like(prev_ref)

    x = x_ref[...].reshape(R, D_MODEL)
    h = (x * _rms_scale(x) * gmix_ref[...]).astype(BF16)
    res = {}

    def hgrn():
        zq = _dot(h, win_ref[:, _O_HQ:_O_HQ + HG_W])
        zf = _dot(h, win_ref[:, _O_HF:_O_HF + HG_W])
        vi = _dot(h, win_ref[:, _O_HI:_O_HI + HG_W])
        zg = _dot(h, win_ref[:, _O_HG:_O_HG + HG_W])
        yield
        q = _silu(zq)
        lb = hgc_ref[0:1, :]
        one_mlb = hgc_ref[1:2, :]
        e = jnp.exp(-jnp.abs(zf))
        rcp = 1.0 / (1.0 + e)
        small = e * rcp
        pos = zf >= 0.0
        logf = jnp.log(lb + one_mlb * jnp.where(pos, rcp, small))
        kk = one_mlb * jnp.where(pos, small, rcp)
        lc = lc_ref[...]
        b = jnp.concatenate([_sel_dot(lc, logf[i * HG_CUM:(i + 1) * HG_CUM]) for i in range(R // HG_CUM)],
                            axis=0)
        yield
        b3 = b.reshape(R // HG_L, HG_L, HG_W)
        blast = jnp.broadcast_to(b3[:, HG_L - 1:HG_L, :], b3.shape).reshape(R, HG_W)
        qd = (q * jnp.exp(b)).astype(BF16)
        ki = kk * jnp.exp(-b)
        ke = (kk * jnp.exp(blast - b)).astype(BF16)
        dec = jnp.exp(blast)
        vib = vi.astype(BF16)
        lane = lax.broadcasted_iota(jnp.int32, (1, HG_W), 1)
        head_lane = [(lane // HG_DK) == hh for hh in range(HG_H)]
        hgmask = hgmask_ref[...] > 0.5
        GR = 128
        scs = []
        for g in range(R // GR):
            rs = slice(g * GR, (g + 1) * GR)
            ki_g = ki[rs]
            kblk = jnp.concatenate([jnp.where(head_lane[hh], ki_g, 0.0) for hh in range(HG_H)], axis=0)
            scs.append(_dot_nt(qd[rs], kblk.astype(BF16)))
        yield
        oi_parts = []
        for g in range(R // GR):
            rs = slice(g * GR, (g + 1) * GR)
            vi_g = vi[rs]
            sc = jnp.where(hgmask, scs[g], 0.0)
            vstk = jnp.concatenate([jnp.where(head_lane[hh], vi_g, 0.0) for hh in range(HG_H)], axis=0)
            oi_parts.append(_dot(sc.astype(BF16), vstk.astype(BF16)))
        o_intra = jnp.concatenate(oi_parts, axis=0)
        yield
        headblk_bf = headblk_ref[...]
        headblk = headblk_bf.astype(F32) > 0.5
        sts = [hst_ref[bi] for bi in range(NB)]
        hoi = [[None] * (TT // HG_L) for _ in range(NB)]
        for cc in range(TT // HG_L):
            dsts = []
            for bi in range(NB):
                r0 = bi * TT + cc * HG_L
                rows = slice(r0, r0 + HG_L)
                hoi[bi][cc] = _dot_nt(qd[rows], sts[bi].astype(BF16))
                dsts.append(_dot_tn(vib[rows], ke[rows]))
            yield
            for bi in range(NB):
                r0 = bi * TT + cc * HG_L
                sts[bi] = dec[r0:r0 + 1, :] * sts[bi] + jnp.where(headblk, dsts[bi], 0.0)
        for bi in range(NB):
            hst_ref[bi] = sts[bi]
        o = o_intra + jnp.concatenate([part for per_b in hoi for part in per_b], axis=0)
        o2 = o * o
        o2_hi = o2.astype(BF16)
        o2_lo = (o2 - o2_hi.astype(F32)).astype(BF16)
        ms_h = (_dot(o2_hi, headblk_bf) + _dot(o2_lo, headblk_bf)) * (1.0 / HG_DK)
        yield
        o_hg = o * lax.rsqrt(ms_h + EPS) * hgng_ref[...] * _silu(zg)
        res['hg'] = _dot(o_hg.astype(BF16), wout_ref[0:HG_W, :])

    def s5():
        u = _dot(h, win_ref[:, _O_SU:_O_SU + S5_W])
        yield
        u_tb = jnp.swapaxes(u.reshape(NB, TT, S5_W), 0, 1).reshape(R, S5_W).astype(BF16)
        yield
        bu_ref[...] = _dot(u_tb, bblk_ref[...]).reshape(TT, NB, 2 * S5_N)
        yield
        ar = s5a_ref[0:1, :]
        ai = s5a_ref[1:2, :]
        x0 = s5st_ref[...]
        xr, xi = x0[:, :S5_N], x0[:, S5_N:]
        for t in range(TT):
            but = bu_ref[t]
            xr, xi = ar * xr - ai * xi + but[:, :S5_N], ar * xi + ai * xr + but[:, S5_N:]
            bu_ref[t] = jnp.concatenate([xr, xi], axis=-1)
            if t % S5_SCAN_STAGE == S5_SCAN_STAGE - 1:
                yield
        s5st_ref[...] = jnp.concatenate([xr, xi], axis=-1)
        xs = bu_ref[...].reshape(R, 2 * S5_N)
        y_tb = _dot(xs.astype(BF16), cblk_ref[...])
        yield
        y = jnp.swapaxes(y_tb.reshape(TT, NB, S5_W), 0, 1).reshape(R, S5_W) + s5d_ref[...] * u
        yield
        y = 0.5 * y * (1.0 + jnp.tanh(math.sqrt(2.0 / math.pi) * (y + 0.044715 * (y * y * y))))
        glu = _dot(y.astype(BF16), wglu_ref[...])
        yield
        o_s5 = y * (0.5 * jnp.tanh(0.5 * glu) + 0.5)
        res['s5'] = _dot(o_s5.astype(BF16), wout_ref[HG_W:HG_W + S5_W, :])

    def mlstm():
        xm = _dot(h, win_ref[:, _O_MX:_O_MX + ML_W])
        yield
        tail = prev_ref[...]
        xm3 = xm.reshape(NB, TT, ML_W)
        prev_ref[...] = xm3[:, TT - 8:, :].reshape(NB * 8, ML_W)
        t8 = lax.broadcasted_iota(jnp.int32, (1, 8, 1), 1)
        conv = xm * convw_ref[CONV_K - 1:CONV_K, :]
        for k in range(1, CONV_K):
            rolled = pltpu.roll(xm, k, 0).reshape(NB, TT, ML_W)
            tail_k = pltpu.roll(tail, NB * 8 - 8 + k, 0).reshape(NB, 8, ML_W)
            head = jnp.where(t8 >= k, rolled[:, 0:8, :], tail_k)
            shifted = jnp.concatenate([head, rolled[:, 8:, :]], axis=1).reshape(R, ML_W)
            conv = conv + shifted * convw_ref[CONV_K - 1 - k:CONV_K - k, :]
        conv = conv + convb_ref[...]
        xc = _silu(conv)
        xcb = xc.astype(BF16)
        xmb = xm.astype(BF16)
        q_hs, k_hs, vt_hs = [], [], []
        for hh in range(ML_H):
            ls = slice(hh * ML_DH, (hh + 1) * ML_DH)
            qk = _dot(xcb[:, ls], wqk_ref[hh])
            q_hs.append(qk[:, :ML_DH])
            k_hs.append(qk[:, ML_DH:] * (ML_DH ** -0.5))
            vt_hs.append(_dot_nt(wvt_ref[hh], xmb[:, ls]))
        yield
        z = _dot(h, win_ref[:, _O_MZ:_O_MZ + ML_W])
        q_all = jnp.concatenate(q_hs, axis=-1).astype(BF16)
        k_all = jnp.concatenate(k_hs, axis=-1).astype(BF16)
        vt_all = jnp.concatenate(vt_hs, axis=0).astype(BF16)
        g = _dot_nt(wgx_ref[...], xcb) + _dot_nt(wgm_ref[...], xmb) + bg_ref[...]
        yield
        pairs = range(NB // 2)
        tls = [slice(p * 128, (p + 1) * 128) for p in pairs]
        l128 = l128_ref[...]
        lsg3 = jnp.concatenate(_split3(_log_sigmoid(g)), axis=0)
        cs = [_dot(lsg3[:, tl], l128) for tl in tls]
        b16 = jnp.concatenate([c[0:16] + c[16:32] + c[32:48] for c in cs], axis=-1)
        yield
        ig8 = g[0:8]
        b8 = b16[8:16]
        v8 = ig8 - b8
        tlane = lax.broadcasted_iota(jnp.int32, (1, R), 1) % TT
        cm = v8
        for sh in (1, 2, 4, 8, 16, 32):
            cm = jnp.maximum(cm, jnp.where(tlane >= sh, pltpu.roll(cm, sh, 1), -jnp.inf))
        yield
        selpair = selpair_ref[...]
        cb3 = jnp.concatenate(_split3(jnp.concatenate([cm, b8], axis=0)), axis=0)
        ss = [_dot(cb3[:, tl], selpair) for tl in tls]
        sb = jnp.concatenate([c[0:16] + c[16:32] + c[32:48] for c in ss], axis=-1)
        yield
        vmax_sb = sb[0:8]
        glast_sb = sb[8:16]
        m_prev_sb = msb_ref[...]
        m_loc_sb = glast_sb + vmax_sb
        m_new_sb = jnp.maximum(glast_sb + m_prev_sb, m_loc_sb)
        sp_sb = jnp.exp(glast_sb + m_prev_sb - m_new_sb)
        sl_sb = jnp.exp(m_loc_sb - m_new_sb)
        msb_ref[...] = m_new_sb
        lane128 = lax.broadcasted_iota(jnp.int32, (1, 128), 1)
        first_half = lane128 < TT

        def per_time(sbarr):
            parts = [jnp.where(first_half, sbarr[:, (2 * p) * 128:(2 * p + 1) * 128],
                               sbarr[:, (2 * p + 1) * 128:(2 * p + 2) * 128]) for p in range(NB // 2)]
            return jnp.concatenate(parts, axis=-1)

        m_prev_t = per_time(m_prev_sb)
        mbig8 = jnp.maximum(cm, m_prev_t)
        scl8 = jnp.exp(m_prev_t - mbig8)
        en8 = jnp.exp(-(b8 + mbig8))
        w2_8 = per_time(sl_sb) * jnp.exp(v8 - per_time(vmax_sb))

        row16 = lax.broadcasted_iota(jnp.int32, (16, 1), 0)
        v_pieces = [piece.astype(F32) for piece in _split3(v8)]
        m_pieces = [piece.astype(F32) for piece in _split3(mbig8)]
        ones_hi = ((row16 >= 3) & (row16 < 6)).astype(F32)
        ones_lo = (row16 < 3).astype(F32)
        mlmask_t = mlmask_t_ref[...] > 0.5
        second_half = jnp.logical_not(first_half)
        n_rows = ((row16 == 0) & first_half) | ((row16 == 1) & second_half)

        for hh in range(ML_H):
            a_op = ones_hi
            b_op = ones_lo
            for i in range(3):
                a_op = a_op + jnp.where(row16 == i, jnp.broadcast_to(v_pieces[i][hh:hh + 1], (16, R)), 0.0)
                b_op = b_op - jnp.where(row16 == 3 + i, jnp.broadcast_to(m_pieces[i][hh:hh + 1], (16, R)), 0.0)
            a_op = a_op.astype(BF16)
            b_op = b_op.astype(BF16)
            ls = slice(hh * ML_DH, (hh + 1) * ML_DH)
            q_ps = [q_all[tl, ls] for tl in tls]
            k_ps = [k_all[tl, ls] for tl in tls]
            vt_ps = [vt_all[ls, tl] for tl in tls]
            logdms = [_dot_tn(a_op[:, tl], b_op[:, tl]) for tl in tls]
            sts = [_dot_nt(k_ps[p], q_ps[p]) for p in pairs]
            yield
            w_ts, num_ts, inter_ts, qns, cts, n_olds = [], [], [], [], [], []
            for p in pairs:
                w_t = sts[p] * jnp.where(mlmask_t, jnp.exp(logdms[p]), 0.0)
                i0 = (2 * p) * ML_H + hh
                ct0 = ct_ref[i0]
                ct1 = ct_ref[i0 + ML_H]
                n_old = n_ref[p * ML_H + hh]
                w_ts.append(w_t)
                num_ts.append(_dot(vt_ps[p], w_t.astype(BF16)))
                inter_ts.append(jnp.where(first_half, _dot_nt(ct0.astype(BF16), q_ps[p]),
                                          _dot_nt(ct1.astype(BF16), q_ps[p])))
                qns.append(_dot_nt(n_old.astype(BF16), q_ps[p]))
                cts.append((ct0, ct1))
                n_olds.append(n_old)
            yield
            for p in pairs:
                tl = tls[p]
                den_r = jnp.sum(w_ts[p], axis=0, keepdims=True)
                qn_r = jnp.where(first_half, qns[p][0:1], qns[p][1:2])
                scl_r = scl8[hh:hh + 1, tl]
                den = den_r + scl_r * qn_r
                inv_r = 1.0 / jnp.maximum(jnp.abs(den), en8[hh:hh + 1, tl])
                h_t = (num_ts[p] + scl_r * inter_ts[p]) * inv_r
                hn_t = h_t * lax.rsqrt(jnp.mean(h_t * h_t, axis=0, keepdims=True) + EPS)
                mh_ref[tl, ls] = hn_t.T
                w2_r = w2_8[hh:hh + 1, tl]
                vtw = vt_ps[p].astype(F32) * w2_r
                dct0 = _dot(jnp.where(first_half, vtw, 0.0).astype(BF16), k_ps[p])
                dct1 = _dot(jnp.where(second_half, vtw, 0.0).astype(BF16), k_ps[p])
                dn = _dot(jnp.where(n_rows, jnp.broadcast_to(w2_r, (16, 128)), 0.0).astype(BF16), k_ps[p])
                sp0 = sp_sb[hh:hh + 1, (2 * p) * 128:(2 * p + 1) * 128]
                sp1 = sp_sb[hh:hh + 1, (2 * p + 1) * 128:(2 * p + 2) * 128]
                i0 = (2 * p) * ML_H + hh
                ct_ref[i0] = sp0 * cts[p][0] + dct0
                ct_ref[i0 + ML_H] = sp1 * cts[p][1] + dct1
                n_ref[p * ML_H + hh] = jnp.where(row16 == 0, sp0, sp1) * n_olds[p] + dn
            yield
        o_ml = (mh_ref[...] * mlng_ref[...] + mlskip_ref[...] * xc) * _silu(z)
        res['ml'] = _dot(o_ml.astype(BF16), wout_ref[HG_W + S5_W:, :])

    _interleave([s5(), mlstm(), hgrn()])

    out = x + (res['hg'] + res['s5'] + res['ml'])
    o_ref[...] = out.reshape(NB, TT, D_MODEL)


def _ffn_kernel(x_ref, g_ref, wg_ref, wu_ref, wd_ref, gfin_ref, o_ref, *, final_norm):
    x = x_ref[...]
    h = (x * _rms_scale(x) * g_ref[...]).astype(BF16)
    acc = x
    for c in range(D_FF // FFN_TF):
        cs = slice(c * FFN_TF, (c + 1) * FFN_TF)
        gate = _dot(h, wg_ref[:, cs])
        up = _dot(h, wu_ref[:, cs])
        act = (_silu(gate) * up).astype(BF16)
        acc = acc + _dot(act, wd_ref[cs, :])
    if final_norm:
        acc = acc * _rms_scale(acc) * gfin_ref[...]
    o_ref[...] = acc


def _np_masks():
    icum = np.arange(HG_CUM)
    lc = ((icum[:, None] // HG_L) == (icum[None, :] // HG_L)) & (icum[None, :] <= icum[:, None])
    i256 = np.arange(HG_W)
    i128 = np.arange(128)
    hg_causal = ((i128[:, None] // HG_L) == (i128[None, :] // HG_L)) & (i128[None, :] <= i128[:, None])
    hgmask = np.tile(hg_causal, (1, HG_H))
    headblk = (i256[:, None] // HG_DK) == (i256[None, :] // HG_DK)
    mlmask_t = ((i128[:, None] // TT) == (i128[None, :] // TT)) & (i128[:, None] <= i128[None, :])
    selpair = np.zeros((128, 256), np.float32)
    for j in range(2):
        selpair[j * TT + TT - 1, j * 128:(j + 1) * 128] = 1.0
    return dict(lc=lc.astype(np.float32), hgmask=hgmask.astype(np.float32),
                headblk=headblk.astype(np.float32), mlmask_t=mlmask_t.astype(np.float32), selpair=selpair)


_MASKS = _np_masks()


def _const_spec(shape):
    nd = len(shape)
    return pl.BlockSpec(shape, lambda *_: (0,) * nd, pipeline_mode=pl.Buffered(1))


class _Layer:
    def __init__(self, stacked, layer):
        self.stacked, self.layer = stacked, layer


def _operand(c):
    if isinstance(c, _Layer):
        layer = c.layer
        zeros = (0,) * (c.stacked.ndim - 1)
        spec = pl.BlockSpec((None,) + c.stacked.shape[1:], lambda *_: (layer,) + zeros,
                            pipeline_mode=pl.Buffered(1))
        return c.stacked, spec
    return c, _const_spec(c.shape)


def _pad_rows(a, rows):
    return jnp.concatenate([a, jnp.zeros((rows - a.shape[0],) + a.shape[1:], a.dtype)], axis=0)


def _s5_params(lam_re, lam_im, log_dt, b_re, b_im, c_re, c_im):
    lre = lam_re.astype(F32)
    lim = lam_im.astype(F32)
    dt = jnp.exp(log_dt.astype(F32))[:, None]
    mag = jnp.exp(lre * dt)
    ang = lim * dt
    ar, ai = mag * jnp.cos(ang), mag * jnp.sin(ang)
    den = lre * lre + lim * lim
    nr, ni = ar - 1.0, ai
    cr = (nr * lre + ni * lim) / den
    ci = (ni * lre - nr * lim) / den
    bbr = cr[..., None] * b_re - ci[..., None] * b_im
    bbi = cr[..., None] * b_im + ci[..., None] * b_re
    eye = jnp.eye(S5_G, dtype=F32)
    blk_r = jnp.einsum('gpc,gk->gckp', bbr, eye).reshape(S5_W, S5_N)
    blk_i = jnp.einsum('gpc,gk->gckp', bbi, eye).reshape(S5_W, S5_N)
    bblk = jnp.concatenate([blk_r, blk_i], axis=1).astype(BF16)
    cb_r = jnp.einsum('gcp,gk->kpgc', c_re, eye).reshape(S5_N, S5_W)
    cb_i = jnp.einsum('gcp,gk->kpgc', c_im, eye).reshape(S5_N, S5_W)
    cblk = jnp.concatenate([cb_r, -cb_i], axis=0).astype(BF16)
    a = _pad_rows(jnp.stack([ar.reshape(S5_N), ai.reshape(S5_N)], axis=0), 8)
    return bblk, cblk, a


def _layer_params(gmix, lb, hg_ng, lam_re, lam_im, log_dt, b_re, b_im, c_re, c_im, s5_d, w_glu, conv_w, conv_b,
                  wq, wk, wv, w_gates, b_gates, ml_ng, ml_skip):
    hgc = _pad_rows(jnp.stack([lb, 1.0 - lb], axis=0), 8)
    bblk, cblk, s5a = _s5_params(lam_re, lam_im, log_dt, b_re, b_im, c_re, c_im)
    wg3 = w_gates.reshape(3, ML_H, ML_DH, 2 * ML_H)
    wgx = (jnp.einsum('hde,hec->hdc', wq, wg3[0]) + (ML_DH ** -0.5) * jnp.einsum('hde,hec->hdc', wk, wg3[1]))
    wgm = jnp.einsum('hde,hec->hdc', wv, wg3[2])
    zrow = jnp.zeros((8 - ML_H, ML_W), F32)

    def gate_rows(w):
        wt = w.reshape(ML_W, 2 * ML_H).T
        return jnp.concatenate([wt[:ML_H], zrow, wt[ML_H:], zrow], axis=0).astype(BF16)

    zb = jnp.zeros((8 - ML_H,), F32)
    bg = jnp.broadcast_to(jnp.concatenate([b_gates[:ML_H], zb, b_gates[ML_H:], zb])[:, None], (16, R))
    return dict(
        gmix=gmix[None, :], hgc=hgc, hg_ng=hg_ng[None, :], bblk=bblk, cblk=cblk, s5a=s5a, s5_d=s5_d[None, :],
        w_glu=w_glu.astype(BF16), conv_w=_pad_rows(conv_w, 8), conv_b=conv_b[None, :],
        wqk=jnp.concatenate([wq, wk], axis=-1).astype(BF16), wvt=jnp.swapaxes(wv, 1, 2).astype(BF16),
        wgx=gate_rows(wgx), wgm=gate_rows(wgm), bg=bg, ml_ng=ml_ng[None, :], ml_skip=ml_skip[None, :])


def _mix_layer(x, layer, lp, w_in, w_out):
    B, S, _ = x.shape
    assert B % NB == 0 and S % TT == 0
    mk = _MASKS
    p = {k: _Layer(v, layer) for k, v in lp.items()}
    consts = [
        p['gmix'], _Layer(w_in, layer),
        p['hgc'], p['hg_ng'], jnp.asarray(mk['lc'], BF16), jnp.asarray(mk['hgmask'], F32),
        jnp.asarray(mk['headblk'], BF16),
        p['bblk'], p['cblk'], p['s5a'], p['s5_d'],
        p['w_glu'],
        p['conv_w'], p['conv_b'], p['wqk'], p['wvt'],
        p['wgx'], p['wgm'], p['bg'],
        jnp.asarray(mk['mlmask_t'], BF16), jnp.asarray(mk['selpair'], BF16), jnp.asarray(mk['mlmask_t'], F32),
        p['ml_ng'], p['ml_skip'], _Layer(w_out, layer),
    ]
    operands = [_operand(c) for c in consts]
    xspec = pl.BlockSpec((NB, TT, D_MODEL), lambda bi, si: (bi, si, 0))
    scratch = [
        pltpu.VMEM((NB, HG_W, HG_W), F32),
        pltpu.VMEM((NB, 2 * S5_N), F32),
        pltpu.VMEM((TT, NB, 2 * S5_N), F32),
        pltpu.VMEM((NB * ML_H, ML_DH, ML_DH), F32),
        pltpu.VMEM((NB // 2 * ML_H, 16, ML_DH), F32),
        pltpu.VMEM((8, NB * 128), F32),
        pltpu.VMEM((NB * 8, ML_W), F32),
        pltpu.VMEM((R, ML_W), F32),
    ]
    return pl.pallas_call(
        _mix_kernel,
        grid=(B // NB, S // TT),
        in_specs=[xspec] + [spec for _, spec in operands],
        out_specs=xspec,
        out_shape=jax.ShapeDtypeStruct(x.shape, F32),
        scratch_shapes=scratch,
        compiler_params=pltpu.CompilerParams(
            dimension_semantics=("parallel", "arbitrary"), vmem_limit_bytes=VMEM_LIMIT),
        name="mixer",
    )(x, *[arr for arr, _ in operands])


def _ffn_layer(x2, g, wg, wu, wd, gfin, final_norm):
    T = x2.shape[0]
    assert T % FFN_TM == 0
    operands = [_operand(c) for c in (g, wg, wu, wd, gfin[None, :])]
    xspec = pl.BlockSpec((FFN_TM, D_MODEL), lambda i: (i, 0))
    return pl.pallas_call(
        functools.partial(_ffn_kernel, final_norm=final_norm),
        grid=(T // FFN_TM,),
        in_specs=[xspec] + [spec for _, spec in operands],
        out_specs=xspec,
        out_shape=jax.ShapeDtypeStruct(x2.shape, F32),
        compiler_params=pltpu.CompilerParams(
            dimension_semantics=("parallel",), vmem_limit_bytes=VMEM_LIMIT),
        name="ffn",
    )(x2, *[arr for arr, _ in operands])


def kernel(x, norm_mix_g, w_in, hgrn_lb_raw, hgrn_norm_g, s5_lambda_re, s5_lambda_im, s5_log_dt, s5_b_re, s5_b_im, s5_c_re, s5_c_im, s5_d, s5_w_glu, ml_conv_w, ml_conv_b, ml_wq, ml_wk, ml_wv, ml_w_gates, ml_b_gates, ml_norm_g, ml_skip, w_out, norm_ffn_g, ffn_w_gate, ffn_w_up, ffn_w_down, norm_final_g):
    B, S, D = x.shape
    depth = w_in.shape[0]
    lb_cs = jnp.cumsum(jax.nn.softmax(hgrn_lb_raw.astype(F32), axis=0), axis=0)
    lb_all = lb_cs - lb_cs[0]
    w_in_b, w_out_b = w_in.astype(BF16), w_out.astype(BF16)
    wg_b, wu_b, wd_b = ffn_w_gate.astype(BF16), ffn_w_up.astype(BF16), ffn_w_down.astype(BF16)
    lp = jax.vmap(_layer_params)(
        norm_mix_g, lb_all, hgrn_norm_g, s5_lambda_re, s5_lambda_im, s5_log_dt, s5_b_re, s5_b_im, s5_c_re,
        s5_c_im, s5_d, s5_w_glu, ml_conv_w, ml_conv_b, ml_wq, ml_wk, ml_wv, ml_w_gates, ml_b_gates,
        ml_norm_g, ml_skip)
    g_ffn = norm_ffn_g[:, None, :]
    for l in range(depth):
        x = _mix_layer(x, l, lp, w_in_b, w_out_b)
        x2 = _ffn_layer(x.reshape(B * S, D), _Layer(g_ffn, l), _Layer(wg_b, l), _Layer(wu_b, l), _Layer(wd_b, l),
                        norm_final_g, final_norm=(l == depth - 1))
        x = x2.reshape(B, S, D)
    return x
```

```python
import functools
import math

import numpy as np
import jax
import jax.numpy as jnp
from jax import lax
from jax.experimental import pallas as pl
from jax.experimental.pallas import tpu as pltpu

F32 = jnp.float32
BF16 = jnp.bfloat16

D_MODEL = 1024
HG_W = 256
HG_H = 4
HG_DK = HG_W // HG_H
HG_L = 32
S5_W = 256
S5_G = 16
S5_GC = 16
S5_P = 64
S5_N = S5_G * S5_P
ML_W = 512
ML_H = 4
ML_DH = ML_W // ML_H
CONV_K = 4
D_FF = 2816
D_IN = 4 * HG_W + S5_W + 2 * ML_W
EPS = 1e-6

NB = 16
HG_CUM = 256
TT = 64
R = NB * TT
FFN_TM = 1024
FFN_TF = 256
S5_SCAN_STAGE = 32
VMEM_LIMIT = 62 * 1024 * 1024

_O_HQ, _O_HF, _O_HI, _O_HG, _O_SU, _O_MX, _O_MZ = 0, 256, 512, 768, 1024, 1280, 1792


def _dot(a, b):
    return jnp.dot(a, b, preferred_element_type=F32)


def _dot_nt(a, b):
    return lax.dot_general(a, b, (((1,), (1,)), ((), ())), preferred_element_type=F32)


def _dot_tn(a, b):
    return lax.dot_general(a, b, (((0,), (0,)), ((), ())), preferred_element_type=F32)


def _split3(x):
    hi = x.astype(BF16)
    r1 = x - hi.astype(F32)
    mid = r1.astype(BF16)
    lo = (r1 - mid.astype(F32)).astype(BF16)
    return hi, mid, lo


def _sel_dot(m01, x):
    return _dot(jnp.concatenate([m01, m01, m01], axis=1), jnp.concatenate(_split3(x), axis=0))


def _silu(v):
    hv = 0.5 * v
    return hv * jnp.tanh(hv) + hv


def _log_sigmoid(v):
    return jnp.minimum(v, 0.0) - jnp.log1p(jnp.exp(-jnp.abs(v)))


def _rms_scale(v):
    return lax.rsqrt(jnp.mean(v * v, axis=-1, keepdims=True) + EPS)


def _interleave(stage_generators):
    live = list(stage_generators)
    while live:
        for g in list(live):
            try:
                next(g)
            except StopIteration:
                live.remove(g)


def _mix_kernel(x_ref, gmix_ref, win_ref,
                hgc_ref, hgng_ref, lc_ref, hgmask_ref, headblk_ref,
                bblk_ref, cblk_ref, s5a_ref, s5d_ref, wglu_ref,
                convw_ref, convb_ref, wqk_ref, wvt_ref, wgx_ref, wgm_ref, bg_ref,
                l128_ref, selpair_ref, mlmask_t_ref, mlng_ref, mlskip_ref, wout_ref,
                o_ref,
                hst_ref, s5st_ref, bu_ref, ct_ref, n_ref, msb_ref, prev_ref, mh_ref):
    si = pl.program_id(1)

    @pl.when(si == 0)
    def _reset_state():
        hst_ref[...] = jnp.zeros_like(hst_ref)
        s5st_ref[...] = jnp.zeros_like(s5st_ref)
        ct_ref[...] = jnp.zeros_like(ct_ref)
        n_ref[...] = jnp.zeros_like(n_ref)
        msb_ref[...] = jnp.zeros_like(msb_ref)
        prev_ref[...] = jnp.zeros_like(prev_ref)

    x = x_ref[...].reshape(R, D_MODEL)
    h = (x * _rms_scale(x) * gmix_ref[...]).astype(BF16)
    res = {}

    def hgrn():
        zq = _dot(h, win_ref[:, _O_HQ:_O_HQ + HG_W])
        zf = _dot(h, win_ref[:, _O_HF:_O_HF + HG_W])
        vi = _dot(h, win_ref[:, _O_HI:_O_HI + HG_W])
        zg = _dot(h, win_ref[:, _O_HG:_O_HG + HG_W])
        yield
        q = _silu(zq)
        lb = hgc_ref[0:1, :]
        one_mlb = hgc_ref[1:2, :]
        e = jnp.exp(-jnp.abs(zf))
        rcp = 1.0 / (1.0 + e)
        small = e * rcp
        pos = zf >= 0.0
        logf = jnp.log(lb + one_mlb * jnp.where(pos, rcp, small))
        kk = one_mlb * jnp.where(pos, small, rcp)
        lc = lc_ref[...]
        b = jnp.concatenate([_sel_dot(lc, logf[i * HG_CUM:(i + 1) * HG_CUM]) for i in range(R // HG_CUM)],
                            axis=0)
        yield
        b3 = b.reshape(R // HG_L, HG_L, HG_W)
        blast = jnp.broadcast_to(b3[:, HG_L - 1:HG_L, :], b3.shape).reshape(R, HG_W)
        qd = (q * jnp.exp(b)).astype(BF16)
        ki = kk * jnp.exp(-b)
        ke = (kk * jnp.exp(blast - b)).astype(BF16)
        dec = jnp.exp(blast)
        vib = vi.astype(BF16)
        lane = lax.broadcasted_iota(jnp.int32, (1, HG_W), 1)
        head_lane = [(lane // HG_DK) == hh for hh in range(HG_H)]
        hgmask = hgmask_ref[...] > 0.5
        GR = 128
        scs = []
        for g in range(R // GR):
            rs = slice(g * GR, (g + 1) * GR)
            ki_g = ki[rs]
            kblk = jnp.concatenate([jnp.where(head_lane[hh], ki_g, 0.0) for hh in range(HG_H)], axis=0)
            scs.append(_dot_nt(qd[rs], kblk.astype(BF16)))
        yield
        oi_parts = []
        for g in range(R // GR):
            rs = slice(g * GR, (g + 1) * GR)
            vi_g = vi[rs]
            sc = jnp.where(hgmask, scs[g], 0.0)
            vstk = jnp.concatenate([jnp.where(head_lane[hh], vi_g, 0.0) for hh in range(HG_H)], axis=0)
            oi_parts.append(_dot(sc.astype(BF16), vstk.astype(BF16)))
        o_intra = jnp.concatenate(oi_parts, axis=0)
        yield
        headblk_bf = headblk_ref[...]
        headblk = headblk_bf.astype(F32) > 0.5
        sts = [hst_ref[bi] for bi in range(NB)]
        hoi = [[None] * (TT // HG_L) for _ in range(NB)]
        for cc in range(TT // HG_L):
            dsts = []
            for bi in range(NB):
                r0 = bi * TT + cc * HG_L
                rows = slice(r0, r0 + HG_L)
                hoi[bi][cc] = _dot_nt(qd[rows], sts[bi].astype(BF16))
                dsts.append(_dot_tn(vib[rows], ke[rows]))
            yield
            for bi in range(NB):
                r0 = bi * TT + cc * HG_L
                sts[bi] = dec[r0:r0 + 1, :] * sts[bi] + jnp.where(headblk, dsts[bi], 0.0)
        for bi in range(NB):
            hst_ref[bi] = sts[bi]
        o = o_intra + jnp.concatenate([part for per_b in hoi for part in per_b], axis=0)
        o2 = o * o
        o2_hi = o2.astype(BF16)
        o2_lo = (o2 - o2_hi.astype(F32)).astype(BF16)
        ms_h = (_dot(o2_hi, headblk_bf) + _dot(o2_lo, headblk_bf)) * (1.0 / HG_DK)
        yield
        o_hg = o * lax.rsqrt(ms_h + EPS) * hgng_ref[...] * _silu(zg)
        res['hg'] = _dot(o_hg.astype(BF16), wout_ref[0:HG_W, :])

    def s5():
        u = _dot(h, win_ref[:, _O_SU:_O_SU + S5_W])
        yield
        u_tb = jnp.swapaxes(u.reshape(NB, TT, S5_W), 0, 1).reshape(R, S5_W).astype(BF16)
        yield
        bu_ref[...] = _dot(u_tb, bblk_ref[...]).reshape(TT, NB, 2 * S5_N)
        yield
        ar = s5a_ref[0:1, :]
        ai = s5a_ref[1:2, :]
        x0 = s5st_ref[...]
        xr, xi = x0[:, :S5_N], x0[:, S5_N:]
        for t in range(TT):
            but = bu_ref[t]
            xr, xi = ar * xr - ai * xi + but[:, :S5_N], ar * xi + ai * xr + but[:, S5_N:]
            bu_ref[t] = jnp.concatenate([xr, xi], axis=-1)
            if t % S5_SCAN_STAGE == S5_SCAN_STAGE - 1:
                yield
        s5st_ref[...] = jnp.concatenate([xr, xi], axis=-1)
        xs = bu_ref[...].reshape(R, 2 * S5_N)
        y_tb = _dot(xs.astype(BF16), cblk_ref[...])
        yield
        y = jnp.swapaxes(y_tb.reshape(TT, NB, S5_W), 0, 1).reshape(R, S5_W) + s5d_ref[...] * u
        yield
        y = 0.5 * y * (1.0 + jnp.tanh(math.sqrt(2.0 / math.pi) * (y + 0.044715 * (y * y * y))))
        glu = _dot(y.astype(BF16), wglu_ref[...])
        yield
        o_s5 = y * (0.5 * jnp.tanh(0.5 * glu) + 0.5)
        res['s5'] = _dot(o_s5.astype(BF16), wout_ref[HG_W:HG_W + S5_W, :])

    def mlstm():
        xm = _dot(h, win_ref[:, _O_MX:_O_MX + ML_W])
        yield
        tail = prev_ref[...]
        xm3 = xm.reshape(NB, TT, ML_W)
        prev_ref[...] = xm3[:, TT - 8:, :].reshape(NB * 8, ML_W)
        t8 = lax.broadcasted_iota(jnp.int32, (1, 8, 1), 1)
        conv = xm * convw_ref[CONV_K - 1:CONV_K, :]
        for k in range(1, CONV_K):
            rolled = pltpu.roll(xm, k, 0).reshape(NB, TT, ML_W)
            tail_k = pltpu.roll(tail, NB * 8 - 8 + k, 0).reshape(NB, 8, ML_W)
            head = jnp.where(t8 >= k, rolled[:, 0:8, :], tail_k)
            shifted = jnp.concatenate([head, rolled[:, 8:, :]], axis=1).reshape(R, ML_W)
            conv = conv + shifted * convw_ref[CONV_K - 1 - k:CONV_K - k, :]
        conv = conv + convb_ref[...]
        xc = _silu(conv)
        xcb = xc.astype(BF16)
        xmb = xm.astype(BF16)
        q_hs, k_hs, vt_hs = [], [], []
        for hh in range(ML_H):
            ls = slice(hh * ML_DH, (hh + 1) * ML_DH)
            qk = _dot(xcb[:, ls], wqk_ref[hh])
            q_hs.append(qk[:, :ML_DH])
            k_hs.append(qk[:, ML_DH:] * (ML_DH ** -0.5))
            vt_hs.append(_dot_nt(wvt_ref[hh], xmb[:, ls]))
        yield
        z = _dot(h, win_ref[:, _O_MZ:_O_MZ + ML_W])
        q_all = jnp.concatenate(q_hs, axis=-1).astype(BF16)
        k_all = jnp.concatenate(k_hs, axis=-1).astype(BF16)
        vt_all = jnp.concatenate(vt_hs, axis=0).astype(BF16)
        g = _dot_nt(wgx_ref[...], xcb) + _dot_nt(wgm_ref[...], xmb) + bg_ref[...]
        yield
        pairs = range(NB // 2)
        tls = [slice(p * 128, (p + 1) * 128) for p in pairs]
        l128 = l128_ref[...]
        lsg3 = jnp.concatenate(_split3(_log_sigmoid(g)), axis=0)
        cs = [_dot(lsg3[:, tl], l128) for tl in tls]
        b16 = jnp.concatenate([c[0:16] + c[16:32] + c[32:48] for c in cs], axis=-1)
        yield
        ig8 = g[0:8]
        b8 = b16[8:16]
        v8 = ig8 - b8
        tlane = lax.broadcasted_iota(jnp.int32, (1, R), 1) % TT
        cm = v8
        for sh in (1, 2, 4, 8, 16, 32):
            cm = jnp.maximum(cm, jnp.where(tlane >= sh, pltpu.roll(cm, sh, 1), -jnp.inf))
        yield
        selpair = selpair_ref[...]
        cb3 = jnp.concatenate(_split3(jnp.concatenate([cm, b8], axis=0)), axis=0)
        ss = [_dot(cb3[:, tl], selpair) for tl in tls]
        sb = jnp.concatenate([c[0:16] + c[16:32] + c[32:48] for c in ss], axis=-1)
        yield
        vmax_sb = sb[0:8]
        glast_sb = sb[8:16]
        m_prev_sb = msb_ref[...]
        m_loc_sb = glast_sb + vmax_sb
        m_new_sb = jnp.maximum(glast_sb + m_prev_sb, m_loc_sb)
        sp_sb = jnp.exp(glast_sb + m_prev_sb - m_new_sb)
        sl_sb = jnp.exp(m_loc_sb - m_new_sb)
        msb_ref[...] = m_new_sb
        lane128 = lax.broadcasted_iota(jnp.int32, (1, 128), 1)
        first_half = lane128 < TT

        def per_time(sbarr):
            parts = [jnp.where(first_half, sbarr[:, (2 * p) * 128:(2 * p + 1) * 128],
                               sbarr[:, (2 * p + 1) * 128:(2 * p + 2) * 128]) for p in range(NB // 2)]
            return jnp.concatenate(parts, axis=-1)

        m_prev_t = per_time(m_prev_sb)
        mbig8 = jnp.maximum(cm, m_prev_t)
        scl8 = jnp.exp(m_prev_t - mbig8)
        en8 = jnp.exp(-(b8 + mbig8))
        w2_8 = per_time(sl_sb) * jnp.exp(v8 - per_time(vmax_sb))

        row16 = lax.broadcasted_iota(jnp.int32, (16, 1), 0)
        v_pieces = [piece.astype(F32) for piece in _split3(v8)]
        m_pieces = [piece.astype(F32) for piece in _split3(mbig8)]
        ones_hi = ((row16 >= 3) & (row16 < 6)).astype(F32)
        ones_lo = (row16 < 3).astype(F32)
        mlmask_t = mlmask_t_ref[...] > 0.5
        second_half = jnp.logical_not(first_half)
        n_rows = ((row16 == 0) & first_half) | ((row16 == 1) & second_half)

        for hh in range(ML_H):
            a_op = ones_hi
            b_op = ones_lo
            for i in range(3):
                a_op = a_op + jnp.where(row16 == i, jnp.broadcast_to(v_pieces[i][hh:hh + 1], (16, R)), 0.0)
                b_op = b_op - jnp.where(row16 == 3 + i, jnp.broadcast_to(m_pieces[i][hh:hh + 1], (16, R)), 0.0)
            a_op = a_op.astype(BF16)
            b_op = b_op.astype(BF16)
            ls = slice(hh * ML_DH, (hh + 1) * ML_DH)
            q_ps = [q_all[tl, ls] for tl in tls]
            k_ps = [k_all[tl, ls] for tl in tls]
            vt_ps = [vt_all[ls, tl] for tl in tls]
            logdms = [_dot_tn(a_op[:, tl], b_op[:, tl]) for tl in tls]
            cts, n_olds, qprods = [], [], []
            for p in pairs:
                i0 = (2 * p) * ML_H + hh
                ct0 = ct_ref[i0]
                ct1 = ct_ref[i0 + ML_H]
                n_old = n_ref[p * ML_H + hh]
                cts.append((ct0, ct1))
                n_olds.append(n_old)
                lhs = jnp.concatenate([k_ps[p], ct0.astype(BF16), ct1.astype(BF16), n_old.astype(BF16)], axis=0)
                qprods.append(_dot_nt(lhs, q_ps[p]))
            yield
            w_ts, num_ts, inter_ts, qns = [], [], [], []
            for p in pairs:
                w_t = qprods[p][0:128] * jnp.where(mlmask_t, jnp.exp(logdms[p]), 0.0)
                w_ts.append(w_t)
                num_ts.append(_dot(vt_ps[p], w_t.astype(BF16)))
                inter_ts.append(jnp.where(first_half, qprods[p][128:256], qprods[p][256:384]))
                qns.append(qprods[p][384:400])
            yield
            for p in pairs:
                tl = tls[p]
                den_r = jnp.sum(w_ts[p], axis=0, keepdims=True)
                qn_r = jnp.where(first_half, qns[p][0:1], qns[p][1:2])
                scl_r = scl8[hh:hh + 1, tl]
                den = den_r + scl_r * qn_r
                inv_r = 1.0 / jnp.maximum(jnp.abs(den), en8[hh:hh + 1, tl])
                h_t = (num_ts[p] + scl_r * inter_ts[p]) * inv_r
                hn_t = h_t * lax.rsqrt(jnp.mean(h_t * h_t, axis=0, keepdims=True) + EPS)
                mh_ref[tl, ls] = hn_t.T
                w2_r = w2_8[hh:hh + 1, tl]
                vtw = vt_ps[p].astype(F32) * w2_r
                lhs = jnp.concatenate([jnp.where(first_half, vtw, 0.0).astype(BF16),
                                       jnp.where(second_half, vtw, 0.0).astype(BF16),
                                       jnp.where(n_rows, jnp.broadcast_to(w2_r, (16, 128)), 0.0).astype(BF16)],
                                      axis=0)
                kprod = _dot(lhs, k_ps[p])
                dct0, dct1, dn = kprod[0:128], kprod[128:256], kprod[256:272]
                sp0 = sp_sb[hh:hh + 1, (2 * p) * 128:(2 * p + 1) * 128]
                sp1 = sp_sb[hh:hh + 1, (2 * p + 1) * 128:(2 * p + 2) * 128]
                i0 = (2 * p) * ML_H + hh
                ct_ref[i0] = sp0 * cts[p][0] + dct0
                ct_ref[i0 + ML_H] = sp1 * cts[p][1] + dct1
                n_ref[p * ML_H + hh] = jnp.where(row16 == 0, sp0, sp1) * n_olds[p] + dn
            yield
        o_ml = (mh_ref[...] * mlng_ref[...] + mlskip_ref[...] * xc) * _silu(z)
        res['ml'] = _dot(o_ml.astype(BF16), wout_ref[HG_W + S5_W:, :])

    _interleave([s5(), mlstm(), hgrn()])

    out = x + (res['hg'] + res['s5'] + res['ml'])
    o_ref[...] = out.reshape(NB, TT, D_MODEL)


def _ffn_kernel(x_ref, g_ref, wg_ref, wu_ref, wd_ref, gfin_ref, o_ref, *, final_norm):
    x = x_ref[...]
    h = (x * _rms_scale(x) * g_ref[...]).astype(BF16)
    acc = x
    for c in range(D_FF // FFN_TF):
        cs = slice(c * FFN_TF, (c + 1) * FFN_TF)
        gate = _dot(h, wg_ref[:, cs])
        up = _dot(h, wu_ref[:, cs])
        act = (_silu(gate) * up).astype(BF16)
        acc = acc + _dot(act, wd_ref[cs, :])
    if final_norm:
        acc = acc * _rms_scale(acc) * gfin_ref[...]
    o_ref[...] = acc


def _np_masks():
    icum = np.arange(HG_CUM)
    lc = ((icum[:, None] // HG_L) == (icum[None, :] // HG_L)) & (icum[None, :] <= icum[:, None])
    i256 = np.arange(HG_W)
    i128 = np.arange(128)
    hg_causal = ((i128[:, None] // HG_L) == (i128[None, :] // HG_L)) & (i128[None, :] <= i128[:, None])
    hgmask = np.tile(hg_causal, (1, HG_H))
    headblk = (i256[:, None] // HG_DK) == (i256[None, :] // HG_DK)
    mlmask_t = ((i128[:, None] // TT) == (i128[None, :] // TT)) & (i128[:, None] <= i128[None, :])
    selpair = np.zeros((128, 256), np.float32)
    for j in range(2):
        selpair[j * TT + TT - 1, j * 128:(j + 1) * 128] = 1.0
    return dict(lc=lc.astype(np.float32), hgmask=hgmask.astype(np.float32),
                headblk=headblk.astype(np.float32), mlmask_t=mlmask_t.astype(np.float32), selpair=selpair)


_MASKS = _np_masks()


def _const_spec(shape):
    nd = len(shape)
    return pl.BlockSpec(shape, lambda *_: (0,) * nd, pipeline_mode=pl.Buffered(1))


class _Layer:
    def __init__(self, stacked, layer):
        self.stacked, self.layer = stacked, layer


def _operand(c):
    if isinstance(c, _Layer):
        layer = c.layer
        zeros = (0,) * (c.stacked.ndim - 1)
        spec = pl.BlockSpec((None,) + c.stacked.shape[1:], lambda *_: (layer,) + zeros,
                            pipeline_mode=pl.Buffered(1))
        return c.stacked, spec
    return c, _const_spec(c.shape)


def _pad_rows(a, rows):
    return jnp.concatenate([a, jnp.zeros((rows - a.shape[0],) + a.shape[1:], a.dtype)], axis=0)


def _s5_params(lam_re, lam_im, log_dt, b_re, b_im, c_re, c_im):
    lre = lam_re.astype(F32)
    lim = lam_im.astype(F32)
    dt = jnp.exp(log_dt.astype(F32))[:, None]
    mag = jnp.exp(lre * dt)
    ang = lim * dt
    ar, ai = mag * jnp.cos(ang), mag * jnp.sin(ang)
    den = lre * lre + lim * lim
    nr, ni = ar - 1.0, ai
    cr = (nr * lre + ni * lim) / den
    ci = (ni * lre - nr * lim) / den
    bbr = cr[..., None] * b_re - ci[..., None] * b_im
    bbi = cr[..., None] * b_im + ci[..., None] * b_re
    eye = jnp.eye(S5_G, dtype=F32)
    blk_r = jnp.einsum('gpc,gk->gckp', bbr, eye).reshape(S5_W, S5_N)
    blk_i = jnp.einsum('gpc,gk->gckp', bbi, eye).reshape(S5_W, S5_N)
    bblk = jnp.concatenate([blk_r, blk_i], axis=1).astype(BF16)
    cb_r = jnp.einsum('gcp,gk->kpgc', c_re, eye).reshape(S5_N, S5_W)
    cb_i = jnp.einsum('gcp,gk->kpgc', c_im, eye).reshape(S5_N, S5_W)
    cblk = jnp.concatenate([cb_r, -cb_i], axis=0).astype(BF16)
    a = _pad_rows(jnp.stack([ar.reshape(S5_N), ai.reshape(S5_N)], axis=0), 8)
    return bblk, cblk, a


def _layer_params(gmix, lb, hg_ng, lam_re, lam_im, log_dt, b_re, b_im, c_re, c_im, s5_d, w_glu, conv_w, conv_b,
                  wq, wk, wv, w_gates, b_gates, ml_ng, ml_skip):
    hgc = _pad_rows(jnp.stack([lb, 1.0 - lb], axis=0), 8)
    bblk, cblk, s5a = _s5_params(lam_re, lam_im, log_dt, b_re, b_im, c_re, c_im)
    wg3 = w_gates.reshape(3, ML_H, ML_DH, 2 * ML_H)
    wgx = (jnp.einsum('hde,hec->hdc', wq, wg3[0]) + (ML_DH ** -0.5) * jnp.einsum('hde,hec->hdc', wk, wg3[1]))
    wgm = jnp.einsum('hde,hec->hdc', wv, wg3[2])
    zrow = jnp.zeros((8 - ML_H, ML_W), F32)

    def gate_rows(w):
        wt = w.reshape(ML_W, 2 * ML_H).T
        return jnp.concatenate([wt[:ML_H], zrow, wt[ML_H:], zrow], axis=0).astype(BF16)

    zb = jnp.zeros((8 - ML_H,), F32)
    bg = jnp.broadcast_to(jnp.concatenate([b_gates[:ML_H], zb, b_gates[ML_H:], zb])[:, None], (16, R))
    return dict(
        gmix=gmix[None, :], hgc=hgc, hg_ng=hg_ng[None, :], bblk=bblk, cblk=cblk, s5a=s5a, s5_d=s5_d[None, :],
        w_glu=w_glu.astype(BF16), conv_w=_pad_rows(conv_w, 8), conv_b=conv_b[None, :],
        wqk=jnp.concatenate([wq, wk], axis=-1).astype(BF16), wvt=jnp.swapaxes(wv, 1, 2).astype(BF16),
        wgx=gate_rows(wgx), wgm=gate_rows(wgm), bg=bg, ml_ng=ml_ng[None, :], ml_skip=ml_skip[None, :])


def _mix_layer(x, layer, lp, w_in, w_out):
    B, S, _ = x.shape
    assert B % NB == 0 and S % TT == 0
    mk = _MASKS
    p = {k: _Layer(v, layer) for k, v in lp.items()}
    consts = [
        p['gmix'], _Layer(w_in, layer),
        p['hgc'], p['hg_ng'], jnp.asarray(mk['lc'], BF16), jnp.asarray(mk['hgmask'], F32),
        jnp.asarray(mk['headblk'], BF16),
        p['bblk'], p['cblk'], p['s5a'], p['s5_d'],
        p['w_glu'],
        p['conv_w'], p['conv_b'], p['wqk'], p['wvt'],
        p['wgx'], p['wgm'], p['bg'],
        jnp.asarray(mk['mlmask_t'], BF16), jnp.asarray(mk['selpair'], BF16), jnp.asarray(mk['mlmask_t'], F32),
        p['ml_ng'], p['ml_skip'], _Layer(w_out, layer),
    ]
    operands = [_operand(c) for c in consts]
    xspec = pl.BlockSpec((NB, TT, D_MODEL), lambda bi, si: (bi, si, 0))
    scratch = [
        pltpu.VMEM((NB, HG_W, HG_W), F32),
        pltpu.VMEM((NB, 2 * S5_N), F32),
        pltpu.VMEM((TT, NB, 2 * S5_N), F32),
        pltpu.VMEM((NB * ML_H, ML_DH, ML_DH), F32),
        pltpu.VMEM((NB // 2 * ML_H, 16, ML_DH), F32),
        pltpu.VMEM((8, NB * 128), F32),
        pltpu.VMEM((NB * 8, ML_W), F32),
        pltpu.VMEM((R, ML_W), F32),
    ]
    return pl.pallas_call(
        _mix_kernel,
        grid=(B // NB, S // TT),
        in_specs=[xspec] + [spec for _, spec in operands],
        out_specs=xspec,
        out_shape=jax.ShapeDtypeStruct(x.shape, F32),
        scratch_shapes=scratch,
        compiler_params=pltpu.CompilerParams(
            dimension_semantics=("parallel", "arbitrary"), vmem_limit_bytes=VMEM_LIMIT),
        name="mixer",
    )(x, *[arr for arr, _ in operands])


def _ffn_layer(x2, g, wg, wu, wd, gfin, final_norm):
    T = x2.shape[0]
    assert T % FFN_TM == 0
    operands = [_operand(c) for c in (g, wg, wu, wd, gfin[None, :])]
    xspec = pl.BlockSpec((FFN_TM, D_MODEL), lambda i: (i, 0))
    return pl.pallas_call(
        functools.partial(_ffn_kernel, final_norm=final_norm),
        grid=(T // FFN_TM,),
        in_specs=[xspec] + [spec for _, spec in operands],
        out_specs=xspec,
        out_shape=jax.ShapeDtypeStruct(x2.shape, F32),
        compiler_params=pltpu.CompilerParams(
            dimension_semantics=("parallel",), vmem_limit_bytes=VMEM_LIMIT),
        name="ffn",
    )(x2, *[arr for arr, _ in operands])


def kernel(x, norm_mix_g, w_in, hgrn_lb_raw, hgrn_norm_g, s5_lambda_re, s5_lambda_im, s5_log_dt, s5_b_re, s5_b_im, s5_c_re, s5_c_im, s5_d, s5_w_glu, ml_conv_w, ml_conv_b, ml_wq, ml_wk, ml_wv, ml_w_gates, ml_b_gates, ml_norm_g, ml_skip, w_out, norm_ffn_g, ffn_w_gate, ffn_w_up, ffn_w_down, norm_final_g):
    B, S, D = x.shape
    depth = w_in.shape[0]
    lb_cs = jnp.cumsum(jax.nn.softmax(hgrn_lb_raw.astype(F32), axis=0), axis=0)
    lb_all = lb_cs - lb_cs[0]
    w_in_b, w_out_b = w_in.astype(BF16), w_out.astype(BF16)
    wg_b, wu_b, wd_b = ffn_w_gate.astype(BF16), ffn_w_up.astype(BF16), ffn_w_down.astype(BF16)
    lp = jax.vmap(_layer_params)(
        norm_mix_g, lb_all, hgrn_norm_g, s5_lambda_re, s5_lambda_im, s5_log_dt, s5_b_re, s5_b_im, s5_c_re,
        s5_c_im, s5_d, s5_w_glu, ml_conv_w, ml_conv_b, ml_wq, ml_wk, ml_wv, ml_w_gates, ml_b_gates,
        ml_norm_g, ml_skip)
    g_ffn = norm_ffn_g[:, None, :]
    for l in range(depth):
        x = _mix_layer(x, l, lp, w_in_b, w_out_b)
        x2 = _ffn_layer(x.reshape(B * S, D), _Layer(g_ffn, l), _Layer(wg_b, l), _Layer(wu_b, l), _Layer(wd_b, l),
                        norm_final_g, final_norm=(l == depth - 1))
        x = x2.reshape(B, S, D)
    return x
```

```python
import functools
import math

import numpy as np
import jax
import jax.numpy as jnp
from jax import lax
from jax.experimental import pallas as pl
from jax.experimental.pallas import tpu as pltpu

F32 = jnp.float32
BF16 = jnp.bfloat16

D_MODEL = 1024
HG_W = 256
HG_H = 4
HG_DK = HG_W // HG_H
HG_L = 32
S5_W = 256
S5_G = 16
S5_GC = 16
S5_P = 64
S5_N = S5_G * S5_P
ML_W = 512
ML_H = 4
ML_DH = ML_W // ML_H
CONV_K = 4
D_FF = 2816
D_IN = 4 * HG_W + S5_W + 2 * ML_W
EPS = 1e-6

NB = 16
HG_CUM = 256
TT = 64
R = NB * TT
FFN_TM = 1024
FFN_TF = 256
S5_SCAN_STAGE = 32
VMEM_LIMIT = 62 * 1024 * 1024

_O_HQ, _O_HF, _O_HI, _O_HG, _O_SU, _O_MX, _O_MZ = 0, 256, 512, 768, 1024, 1280, 1792


def _dot(a, b):
    return jnp.dot(a, b, preferred_element_type=F32)


def _dot_nt(a, b):
    return lax.dot_general(a, b, (((1,), (1,)), ((), ())), preferred_element_type=F32)


def _dot_tn(a, b):
    return lax.dot_general(a, b, (((0,), (0,)), ((), ())), preferred_element_type=F32)


def _split3(x):
    hi = x.astype(BF16)
    r1 = x - hi.astype(F32)
    mid = r1.astype(BF16)
    lo = (r1 - mid.astype(F32)).astype(BF16)
    return hi, mid, lo


def _sel_dot(m01, x):
    return _dot(jnp.concatenate([m01, m01, m01], axis=1), jnp.concatenate(_split3(x), axis=0))


def _silu(v):
    hv = 0.5 * v
    return hv * jnp.tanh(hv) + hv


def _log_sigmoid(v):
    return jnp.minimum(v, 0.0) - jnp.log1p(jnp.exp(-jnp.abs(v)))


def _rms_scale(v):
    return lax.rsqrt(jnp.mean(v * v, axis=-1, keepdims=True) + EPS)


def _interleave(stage_generators):
    live = list(stage_generators)
    while live:
        for g in list(live):
            try:
                next(g)
            except StopIteration:
                live.remove(g)


def _mix_kernel(x_ref, gmix_ref, win_ref,
                hgc_ref, hgng_ref, lc_ref, hgmask_ref, headblk_ref,
                bblk_ref, cblk_ref, s5a_ref, s5d_ref, wglu_ref,
                convw_ref, convb_ref, wqk_ref, wvt_ref, wgx_ref, wgm_ref, bg_ref,
                l128_ref, selpair_ref, mlmask_t_ref, mlng_ref, mlskip_ref, wout_ref,
                o_ref,
                hst_ref, s5st_ref, bu_ref, ct_ref, n_ref, msb_ref, prev_ref, mh_ref):
    si = pl.program_id(1)

    @pl.when(si == 0)
    def _reset_state():
        hst_ref[...] = jnp.zeros_like(hst_ref)
        s5st_ref[...] = jnp.zeros_like(s5st_ref)
        ct_ref[...] = jnp.zeros_like(ct_ref)
        n_ref[...] = jnp.zeros_like(n_ref)
        msb_ref[...] = jnp.zeros_like(msb_ref)
        prev_ref[...] = jnp.zeros_like(prev_ref)

    x = x_ref[...].reshape(R, D_MODEL)
    h = (x * _rms_scale(x) * gmix_ref[...]).astype(BF16)
    res = {}

    def hgrn():
        zall = _dot(h, win_ref[:, _O_HQ:_O_HQ + 4 * HG_W])
        zq, zf, vi, zg = (zall[:, i * HG_W:(i + 1) * HG_W] for i in range(4))
        yield
        q = _silu(zq)
        lb = hgc_ref[0:1, :]
        one_mlb = hgc_ref[1:2, :]
        e = jnp.exp(-jnp.abs(zf))
        rcp = 1.0 / (1.0 + e)
        small = e * rcp
        pos = zf >= 0.0
        logf = jnp.log(lb + one_mlb * jnp.where(pos, rcp, small))
        kk = one_mlb * jnp.where(pos, small, rcp)
        lc = lc_ref[...]
        b = jnp.concatenate([_sel_dot(lc, logf[i * HG_CUM:(i + 1) * HG_CUM]) for i in range(R // HG_CUM)],
                            axis=0)
        yield
        b3 = b.reshape(R // HG_L, HG_L, HG_W)
        blast = jnp.broadcast_to(b3[:, HG_L - 1:HG_L, :], b3.shape).reshape(R, HG_W)
        qd = (q * jnp.exp(b)).astype(BF16)
        ki = kk * jnp.exp(-b)
        ke = (kk * jnp.exp(blast - b)).astype(BF16)
        dec = jnp.exp(blast)
        vib = vi.astype(BF16)
        lane = lax.broadcasted_iota(jnp.int32, (1, HG_W), 1)
        head_lane = [(lane // HG_DK) == hh for hh in range(HG_H)]
        hgmask = hgmask_ref[...] > 0.5
        GR = 128
        scs = []
        for g in range(R // GR):
            rs = slice(g * GR, (g + 1) * GR)
            ki_g = ki[rs]
            kblk = jnp.concatenate([jnp.where(head_lane[hh], ki_g, 0.0) for hh in range(HG_H)], axis=0)
            scs.append(_dot_nt(qd[rs], kblk.astype(BF16)))
        yield
        oi_parts = []
        for g in range(R // GR):
            rs = slice(g * GR, (g + 1) * GR)
            vi_g = vi[rs]
            sc = jnp.where(hgmask, scs[g], 0.0)
            vstk = jnp.concatenate([jnp.where(head_lane[hh], vi_g, 0.0) for hh in range(HG_H)], axis=0)
            oi_parts.append(_dot(sc.astype(BF16), vstk.astype(BF16)))
        o_intra = jnp.concatenate(oi_parts, axis=0)
        yield
        headblk_bf = headblk_ref[...]
        headblk = headblk_bf.astype(F32) > 0.5
        sts = [hst_ref[bi] for bi in range(NB)]
        hoi = [[None] * (TT // HG_L) for _ in range(NB)]
        for cc in range(TT // HG_L):
            dsts = []
            for bi in range(NB):
                r0 = bi * TT + cc * HG_L
                rows = slice(r0, r0 + HG_L)
                hoi[bi][cc] = _dot_nt(qd[rows], sts[bi].astype(BF16))
                dsts.append(_dot_tn(vib[rows], ke[rows]))
            yield
            for bi in range(NB):
                r0 = bi * TT + cc * HG_L
                sts[bi] = dec[r0:r0 + 1, :] * sts[bi] + jnp.where(headblk, dsts[bi], 0.0)
        for bi in range(NB):
            hst_ref[bi] = sts[bi]
        o = o_intra + jnp.concatenate([part for per_b in hoi for part in per_b], axis=0)
        o2 = o * o
        o2_hi = o2.astype(BF16)
        o2_lo = (o2 - o2_hi.astype(F32)).astype(BF16)
        ms_h = (_dot(o2_hi, headblk_bf) + _dot(o2_lo, headblk_bf)) * (1.0 / HG_DK)
        yield
        o_hg = o * lax.rsqrt(ms_h + EPS) * hgng_ref[...] * _silu(zg)
        res['hg'] = _dot(o_hg.astype(BF16), wout_ref[0:HG_W, :])

    def s5():
        u = _dot(h, win_ref[:, _O_SU:_O_SU + S5_W])
        yield
        u_tb = jnp.swapaxes(u.reshape(NB, TT, S5_W), 0, 1).reshape(R, S5_W).astype(BF16)
        yield
        bu_ref[...] = _dot(u_tb, bblk_ref[...]).reshape(TT, NB, 2 * S5_N)
        yield
        ar = s5a_ref[0:1, :]
        ai = s5a_ref[1:2, :]
        x0 = s5st_ref[...]
        xr, xi = x0[:, :S5_N], x0[:, S5_N:]
        for t in range(TT):
            but = bu_ref[t]
            xr, xi = ar * xr - ai * xi + but[:, :S5_N], ar * xi + ai * xr + but[:, S5_N:]
            bu_ref[t] = jnp.concatenate([xr, xi], axis=-1)
            if t % S5_SCAN_STAGE == S5_SCAN_STAGE - 1:
                yield
        s5st_ref[...] = jnp.concatenate([xr, xi], axis=-1)
        xs = bu_ref[...].reshape(R, 2 * S5_N)
        y_tb = _dot(xs.astype(BF16), cblk_ref[...])
        yield
        y = jnp.swapaxes(y_tb.reshape(TT, NB, S5_W), 0, 1).reshape(R, S5_W) + s5d_ref[...] * u
        yield
        y = 0.5 * y * (1.0 + jnp.tanh(math.sqrt(2.0 / math.pi) * (y + 0.044715 * (y * y * y))))
        glu = _dot(y.astype(BF16), wglu_ref[...])
        yield
        o_s5 = y * (0.5 * jnp.tanh(0.5 * glu) + 0.5)
        res['s5'] = _dot(o_s5.astype(BF16), wout_ref[HG_W:HG_W + S5_W, :])

    def mlstm():
        xm = _dot(h, win_ref[:, _O_MX:_O_MX + ML_W])
        yield
        tail = prev_ref[...]
        xm3 = xm.reshape(NB, TT, ML_W)
        prev_ref[...] = xm3[:, TT - 8:, :].reshape(NB * 8, ML_W)
        t8 = lax.broadcasted_iota(jnp.int32, (1, 8, 1), 1)
        conv = xm * convw_ref[CONV_K - 1:CONV_K, :]
        for k in range(1, CONV_K):
            rolled = pltpu.roll(xm, k, 0).reshape(NB, TT, ML_W)
            tail_k = pltpu.roll(tail, NB * 8 - 8 + k, 0).reshape(NB, 8, ML_W)
            head = jnp.where(t8 >= k, rolled[:, 0:8, :], tail_k)
            shifted = jnp.concatenate([head, rolled[:, 8:, :]], axis=1).reshape(R, ML_W)
            conv = conv + shifted * convw_ref[CONV_K - 1 - k:CONV_K - k, :]
        conv = conv + convb_ref[...]
        xc = _silu(conv)
        xcb = xc.astype(BF16)
        xmb = xm.astype(BF16)
        q_hs, k_hs, vt_hs = [], [], []
        for hh in range(ML_H):
            ls = slice(hh * ML_DH, (hh + 1) * ML_DH)
            qk = _dot(xcb[:, ls], wqk_ref[hh])
            q_hs.append(qk[:, :ML_DH])
            k_hs.append(qk[:, ML_DH:] * (ML_DH ** -0.5))
            vt_hs.append(_dot_nt(wvt_ref[hh], xmb[:, ls]))
        yield
        z = _dot(h, win_ref[:, _O_MZ:_O_MZ + ML_W])
        q_all = jnp.concatenate(q_hs, axis=-1).astype(BF16)
        k_all = jnp.concatenate(k_hs, axis=-1).astype(BF16)
        vt_all = jnp.concatenate(vt_hs, axis=0).astype(BF16)
        g = _dot_nt(wgx_ref[...], xcb) + _dot_nt(wgm_ref[...], xmb) + bg_ref[...]
        yield
        pairs = range(NB // 2)
        tls = [slice(p * 128, (p + 1) * 128) for p in pairs]
        l128 = l128_ref[...]
        lsg3 = jnp.concatenate(_split3(_log_sigmoid(g)), axis=0)
        cs = _dot(jnp.concatenate([lsg3[:, tl] for tl in tls], axis=0), l128)
        b16 = jnp.concatenate([cs[48 * p:48 * p + 16] + cs[48 * p + 16:48 * p + 32] + cs[48 * p + 32:48 * p + 48]
                               for p in pairs], axis=-1)
        yield
        ig8 = g[0:8]
        b8 = b16[8:16]
        v8 = ig8 - b8
        tlane = lax.broadcasted_iota(jnp.int32, (1, R), 1) % TT
        cm = v8
        for sh in (1, 2, 4, 8, 16, 32):
            cm = jnp.maximum(cm, jnp.where(tlane >= sh, pltpu.roll(cm, sh, 1), -jnp.inf))
        yield
        selpair = selpair_ref[...]
        cb3 = jnp.concatenate(_split3(jnp.concatenate([cm, b8], axis=0)), axis=0)
        ss = _dot(jnp.concatenate([cb3[:, tl] for tl in tls], axis=0), selpair)
        sb = jnp.concatenate([ss[48 * p:48 * p + 16] + ss[48 * p + 16:48 * p + 32] + ss[48 * p + 32:48 * p + 48]
                              for p in pairs], axis=-1)
        yield
        vmax_sb = sb[0:8]
        glast_sb = sb[8:16]
        m_prev_sb = msb_ref[...]
        m_loc_sb = glast_sb + vmax_sb
        m_new_sb = jnp.maximum(glast_sb + m_prev_sb, m_loc_sb)
        sp_sb = jnp.exp(glast_sb + m_prev_sb - m_new_sb)
        sl_sb = jnp.exp(m_loc_sb - m_new_sb)
        msb_ref[...] = m_new_sb
        lane128 = lax.broadcasted_iota(jnp.int32, (1, 128), 1)
        first_half = lane128 < TT

        def per_time(sbarr):
            parts = [jnp.where(first_half, sbarr[:, (2 * p) * 128:(2 * p + 1) * 128],
                               sbarr[:, (2 * p + 1) * 128:(2 * p + 2) * 128]) for p in range(NB // 2)]
            return jnp.concatenate(parts, axis=-1)

        m_prev_t = per_time(m_prev_sb)
        mbig8 = jnp.maximum(cm, m_prev_t)
        scl8 = jnp.exp(m_prev_t - mbig8)
        en8 = jnp.exp(-(b8 + mbig8))
        w2_8 = per_time(sl_sb) * jnp.exp(v8 - per_time(vmax_sb))

        row16 = lax.broadcasted_iota(jnp.int32, (16, 1), 0)
        v_pieces = [piece.astype(F32) for piece in _split3(v8)]
        m_pieces = [piece.astype(F32) for piece in _split3(mbig8)]
        ones_hi = ((row16 >= 3) & (row16 < 6)).astype(F32)
        ones_lo = (row16 < 3).astype(F32)
        mlmask_t = mlmask_t_ref[...] > 0.5
        second_half = jnp.logical_not(first_half)
        n_rows = ((row16 == 0) & first_half) | ((row16 == 1) & second_half)

        for hh in range(ML_H):
            a_op = ones_hi
            b_op = ones_lo
            for i in range(3):
                a_op = a_op + jnp.where(row16 == i, jnp.broadcast_to(v_pieces[i][hh:hh + 1], (16, R)), 0.0)
                b_op = b_op - jnp.where(row16 == 3 + i, jnp.broadcast_to(m_pieces[i][hh:hh + 1], (16, R)), 0.0)
            a_op = a_op.astype(BF16)
            b_op = b_op.astype(BF16)
            ls = slice(hh * ML_DH, (hh + 1) * ML_DH)
            q_ps = [q_all[tl, ls] for tl in tls]
            k_ps = [k_all[tl, ls] for tl in tls]
            vt_ps = [vt_all[ls, tl] for tl in tls]
            logdms = [_dot_tn(a_op[:, tl], b_op[:, tl]) for tl in tls]
            cts, n_olds, qprods = [], [], []
            for p in pairs:
                i0 = (2 * p) * ML_H + hh
                ct0 = ct_ref[i0]
                ct1 = ct_ref[i0 + ML_H]
                n_old = n_ref[p * ML_H + hh]
                cts.append((ct0, ct1))
                n_olds.append(n_old)
                lhs = jnp.concatenate([k_ps[p], ct0.astype(BF16), ct1.astype(BF16), n_old.astype(BF16)], axis=0)
                qprods.append(_dot_nt(lhs, q_ps[p]))
            yield
            w_ts, num_ts, inter_ts, qns = [], [], [], []
            for p in pairs:
                w_t = qprods[p][0:128] * jnp.where(mlmask_t, jnp.exp(logdms[p]), 0.0)
                w_ts.append(w_t)
                num_ts.append(_dot(vt_ps[p], w_t.astype(BF16)))
                inter_ts.append(jnp.where(first_half, qprods[p][128:256], qprods[p][256:384]))
                qns.append(qprods[p][384:400])
            yield
            for p in pairs:
                tl = tls[p]
                den_r = jnp.sum(w_ts[p], axis=0, keepdims=True)
                qn_r = jnp.where(first_half, qns[p][0:1], qns[p][1:2])
                scl_r = scl8[hh:hh + 1, tl]
                den = den_r + scl_r * qn_r
                inv_r = 1.0 / jnp.maximum(jnp.abs(den), en8[hh:hh + 1, tl])
                h_t = (num_ts[p] + scl_r * inter_ts[p]) * inv_r
                hn_t = h_t * lax.rsqrt(jnp.mean(h_t * h_t, axis=0, keepdims=True) + EPS)
                mh_ref[tl, ls] = hn_t.T
                w2_r = w2_8[hh:hh + 1, tl]
                vtw = vt_ps[p].astype(F32) * w2_r
                lhs = jnp.concatenate([jnp.where(first_half, vtw, 0.0).astype(BF16),
                                       jnp.where(second_half, vtw, 0.0).astype(BF16),
                                       jnp.where(n_rows, jnp.broadcast_to(w2_r, (16, 128)), 0.0).astype(BF16)],
                                      axis=0)
                kprod = _dot(lhs, k_ps[p])
                dct0, dct1, dn = kprod[0:128], kprod[128:256], kprod[256:272]
                sp0 = sp_sb[hh:hh + 1, (2 * p) * 128:(2 * p + 1) * 128]
                sp1 = sp_sb[hh:hh + 1, (2 * p + 1) * 128:(2 * p + 2) * 128]
                i0 = (2 * p) * ML_H + hh
                ct_ref[i0] = sp0 * cts[p][0] + dct0
                ct_ref[i0 + ML_H] = sp1 * cts[p][1] + dct1
                n_ref[p * ML_H + hh] = jnp.where(row16 == 0, sp0, sp1) * n_olds[p] + dn
            yield
        o_ml = (mh_ref[...] * mlng_ref[...] + mlskip_ref[...] * xc) * _silu(z)
        res['ml'] = _dot(o_ml.astype(BF16), wout_ref[HG_W + S5_W:, :])

    _interleave([s5(), mlstm(), hgrn()])

    out = x + (res['hg'] + res['s5'] + res['ml'])
    o_ref[...] = out.reshape(NB, TT, D_MODEL)


def _ffn_kernel(x_ref, g_ref, wg_ref, wu_ref, wd_ref, gfin_ref, o_ref, *, final_norm):
    x = x_ref[...]
    h = (x * _rms_scale(x) * g_ref[...]).astype(BF16)
    acc = x
    for c in range(D_FF // FFN_TF):
        cs = slice(c * FFN_TF, (c + 1) * FFN_TF)
        gate = _dot(h, wg_ref[:, cs])
        up = _dot(h, wu_ref[:, cs])
        act = (_silu(gate) * up).astype(BF16)
        acc = acc + _dot(act, wd_ref[cs, :])
    if final_norm:
        acc = acc * _rms_scale(acc) * gfin_ref[...]
    o_ref[...] = acc


def _np_masks():
    icum = np.arange(HG_CUM)
    lc = ((icum[:, None] // HG_L) == (icum[None, :] // HG_L)) & (icum[None, :] <= icum[:, None])
    i256 = np.arange(HG_W)
    i128 = np.arange(128)
    hg_causal = ((i128[:, None] // HG_L) == (i128[None, :] // HG_L)) & (i128[None, :] <= i128[:, None])
    hgmask = np.tile(hg_causal, (1, HG_H))
    headblk = (i256[:, None] // HG_DK) == (i256[None, :] // HG_DK)
    mlmask_t = ((i128[:, None] // TT) == (i128[None, :] // TT)) & (i128[:, None] <= i128[None, :])
    selpair = np.zeros((128, 256), np.float32)
    for j in range(2):
        selpair[j * TT + TT - 1, j * 128:(j + 1) * 128] = 1.0
    return dict(lc=lc.astype(np.float32), hgmask=hgmask.astype(np.float32),
                headblk=headblk.astype(np.float32), mlmask_t=mlmask_t.astype(np.float32), selpair=selpair)


_MASKS = _np_masks()


def _const_spec(shape):
    nd = len(shape)
    return pl.BlockSpec(shape, lambda *_: (0,) * nd, pipeline_mode=pl.Buffered(1))


class _Layer:
    def __init__(self, stacked, layer):
        self.stacked, self.layer = stacked, layer


def _operand(c):
    if isinstance(c, _Layer):
        layer = c.layer
        zeros = (0,) * (c.stacked.ndim - 1)
        spec = pl.BlockSpec((None,) + c.stacked.shape[1:], lambda *_: (layer,) + zeros,
                            pipeline_mode=pl.Buffered(1))
        return c.stacked, spec
    return c, _const_spec(c.shape)


def _pad_rows(a, rows):
    return jnp.concatenate([a, jnp.zeros((rows - a.shape[0],) + a.shape[1:], a.dtype)], axis=0)


def _s5_params(lam_re, lam_im, log_dt, b_re, b_im, c_re, c_im):
    lre = lam_re.astype(F32)
    lim = lam_im.astype(F32)
    dt = jnp.exp(log_dt.astype(F32))[:, None]
    mag = jnp.exp(lre * dt)
    ang = lim * dt
    ar, ai = mag * jnp.cos(ang), mag * jnp.sin(ang)
    den = lre * lre + lim * lim
    nr, ni = ar - 1.0, ai
    cr = (nr * lre + ni * lim) / den
    ci = (ni * lre - nr * lim) / den
    bbr = cr[..., None] * b_re - ci[..., None] * b_im
    bbi = cr[..., None] * b_im + ci[..., None] * b_re
    eye = jnp.eye(S5_G, dtype=F32)
    blk_r = jnp.einsum('gpc,gk->gckp', bbr, eye).reshape(S5_W, S5_N)
    blk_i = jnp.einsum('gpc,gk->gckp', bbi, eye).reshape(S5_W, S5_N)
    bblk = jnp.concatenate([blk_r, blk_i], axis=1).astype(BF16)
    cb_r = jnp.einsum('gcp,gk->kpgc', c_re, eye).reshape(S5_N, S5_W)
    cb_i = jnp.einsum('gcp,gk->kpgc', c_im, eye).reshape(S5_N, S5_W)
    cblk = jnp.concatenate([cb_r, -cb_i], axis=0).astype(BF16)
    a = _pad_rows(jnp.stack([ar.reshape(S5_N), ai.reshape(S5_N)], axis=0), 8)
    return bblk, cblk, a


def _layer_params(gmix, lb, hg_ng, lam_re, lam_im, log_dt, b_re, b_im, c_re, c_im, s5_d, w_glu, conv_w, conv_b,
                  wq, wk, wv, w_gates, b_gates, ml_ng, ml_skip):
    hgc = _pad_rows(jnp.stack([lb, 1.0 - lb], axis=0), 8)
    bblk, cblk, s5a = _s5_params(lam_re, lam_im, log_dt, b_re, b_im, c_re, c_im)
    wg3 = w_gates.reshape(3, ML_H, ML_DH, 2 * ML_H)
    wgx = (jnp.einsum('hde,hec->hdc', wq, wg3[0]) + (ML_DH ** -0.5) * jnp.einsum('hde,hec->hdc', wk, wg3[1]))
    wgm = jnp.einsum('hde,hec->hdc', wv, wg3[2])
    zrow = jnp.zeros((8 - ML_H, ML_W), F32)

    def gate_rows(w):
        wt = w.reshape(ML_W, 2 * ML_H).T
        return jnp.concatenate([wt[:ML_H], zrow, wt[ML_H:], zrow], axis=0).astype(BF16)

    zb = jnp.zeros((8 - ML_H,), F32)
    bg = jnp.broadcast_to(jnp.concatenate([b_gates[:ML_H], zb, b_gates[ML_H:], zb])[:, None], (16, R))
    return dict(
        gmix=gmix[None, :], hgc=hgc, hg_ng=hg_ng[None, :], bblk=bblk, cblk=cblk, s5a=s5a, s5_d=s5_d[None, :],
        w_glu=w_glu.astype(BF16), conv_w=_pad_rows(conv_w, 8), conv_b=conv_b[None, :],
        wqk=jnp.concatenate([wq, wk], axis=-1).astype(BF16), wvt=jnp.swapaxes(wv, 1, 2).astype(BF16),
        wgx=gate_rows(wgx), wgm=gate_rows(wgm), bg=bg, ml_ng=ml_ng[None, :], ml_skip=ml_skip[None, :])


def _mix_layer(x, layer, lp, w_in, w_out):
    B, S, _ = x.shape
    assert B % NB == 0 and S % TT == 0
    mk = _MASKS
    p = {k: _Layer(v, layer) for k, v in lp.items()}
    consts = [
        p['gmix'], _Layer(w_in, layer),
        p['hgc'], p['hg_ng'], jnp.asarray(mk['lc'], BF16), jnp.asarray(mk['hgmask'], F32),
        jnp.asarray(mk['headblk'], BF16),
        p['bblk'], p['cblk'], p['s5a'], p['s5_d'],
        p['w_glu'],
        p['conv_w'], p['conv_b'], p['wqk'], p['wvt'],
        p['wgx'], p['wgm'], p['bg'],
        jnp.asarray(mk['mlmask_t'], BF16), jnp.asarray(mk['selpair'], BF16), jnp.asarray(mk['mlmask_t'], F32),
        p['ml_ng'], p['ml_skip'], _Layer(w_out, layer),
    ]
    operands = [_operand(c) for c in consts]
    xspec = pl.BlockSpec((NB, TT, D_MODEL), lambda bi, si: (bi, si, 0))
    scratch = [
        pltpu.VMEM((NB, HG_W, HG_W), F32),
        pltpu.VMEM((NB, 2 * S5_N), F32),
        pltpu.VMEM((TT, NB, 2 * S5_N), F32),
        pltpu.VMEM((NB * ML_H, ML_DH, ML_DH), F32),
        pltpu.VMEM((NB // 2 * ML_H, 16, ML_DH), F32),
        pltpu.VMEM((8, NB * 128), F32),
        pltpu.VMEM((NB * 8, ML_W), F32),
        pltpu.VMEM((R, ML_W), F32),
    ]
    return pl.pallas_call(
        _mix_kernel,
        grid=(B // NB, S // TT),
        in_specs=[xspec] + [spec for _, spec in operands],
        out_specs=xspec,
        out_shape=jax.ShapeDtypeStruct(x.shape, F32),
        scratch_shapes=scratch,
        compiler_params=pltpu.CompilerParams(
            dimension_semantics=("parallel", "arbitrary"), vmem_limit_bytes=VMEM_LIMIT),
        name="mixer",
    )(x, *[arr for arr, _ in operands])


def _ffn_layer(x2, g, wg, wu, wd, gfin, final_norm):
    T = x2.shape[0]
    assert T % FFN_TM == 0
    operands = [_operand(c) for c in (g, wg, wu, wd, gfin[None, :])]
    xspec = pl.BlockSpec((FFN_TM, D_MODEL), lambda i: (i, 0))
    return pl.pallas_call(
        functools.partial(_ffn_kernel, final_norm=final_norm),
        grid=(T // FFN_TM,),
        in_specs=[xspec] + [spec for _, spec in operands],
        out_specs=xspec,
        out_shape=jax.ShapeDtypeStruct(x2.shape, F32),
        compiler_params=pltpu.CompilerParams(
            dimension_semantics=("parallel",), vmem_limit_bytes=VMEM_LIMIT),
        name="ffn",
    )(x2, *[arr for arr, _ in operands])


def kernel(x, norm_mix_g, w_in, hgrn_lb_raw, hgrn_norm_g, s5_lambda_re, s5_lambda_im, s5_log_dt, s5_b_re, s5_b_im, s5_c_re, s5_c_im, s5_d, s5_w_glu, ml_conv_w, ml_conv_b, ml_wq, ml_wk, ml_wv, ml_w_gates, ml_b_gates, ml_norm_g, ml_skip, w_out, norm_ffn_g, ffn_w_gate, ffn_w_up, ffn_w_down, norm_final_g):
    B, S, D = x.shape
    depth = w_in.shape[0]
    lb_cs = jnp.cumsum(jax.nn.softmax(hgrn_lb_raw.astype(F32), axis=0), axis=0)
    lb_all = lb_cs - lb_cs[0]
    w_in_b, w_out_b = w_in.astype(BF16), w_out.astype(BF16)
    wg_b, wu_b, wd_b = ffn_w_gate.astype(BF16), ffn_w_up.astype(BF16), ffn_w_down.astype(BF16)
    lp = jax.vmap(_layer_params)(
        norm_mix_g, lb_all, hgrn_norm_g, s5_lambda_re, s5_lambda_im, s5_log_dt, s5_b_re, s5_b_im, s5_c_re,
        s5_c_im, s5_d, s5_w_glu, ml_conv_w, ml_conv_b, ml_wq, ml_wk, ml_wv, ml_w_gates, ml_b_gates,
        ml_norm_g, ml_skip)
    g_ffn = norm_ffn_g[:, None, :]
    for l in range(depth):
        x = _mix_layer(x, l, lp, w_in_b, w_out_b)
        x2 = _ffn_layer(x.reshape(B * S, D), _Layer(g_ffn, l), _Layer(wg_b, l), _Layer(wu_b, l), _Layer(wd_b, l),
                        norm_final_g, final_norm=(l == depth - 1))
        x = x2.reshape(B, S, D)
    return x
```

```python
import functools
import math

import numpy as np
import jax
import jax.numpy as jnp
from jax import lax
from jax.experimental import pallas as pl
from jax.experimental.pallas import tpu as pltpu

F32 = jnp.float32
BF16 = jnp.bfloat16

D_MODEL = 1024
HG_W = 256
HG_H = 4
HG_DK = HG_W // HG_H
HG_L = 32
S5_W = 256
S5_G = 16
S5_GC = 16
S5_P = 64
S5_N = S5_G * S5_P
ML_W = 512
ML_H = 4
ML_DH = ML_W // ML_H
CONV_K = 4
D_FF = 2816
D_IN = 4 * HG_W + S5_W + 2 * ML_W
EPS = 1e-6

NB = 16
HG_CUM = 256
TT = 64
R = NB * TT
FFN_TM = 1024
FFN_TF = 256
S5_SCAN_STAGE = 32
VMEM_LIMIT = 62 * 1024 * 1024

_O_HQ, _O_HF, _O_HI, _O_HG, _O_SU, _O_MX, _O_MZ = 0, 256, 512, 768, 1024, 1280, 1792


def _dot(a, b):
    return jnp.dot(a, b, preferred_element_type=F32)


def _dot_nt(a, b):
    return lax.dot_general(a, b, (((1,), (1,)), ((), ())), preferred_element_type=F32)


def _dot_tn(a, b):
    return lax.dot_general(a, b, (((0,), (0,)), ((), ())), preferred_element_type=F32)


def _split3(x):
    hi = x.astype(BF16)
    r1 = x - hi.astype(F32)
    mid = r1.astype(BF16)
    lo = (r1 - mid.astype(F32)).astype(BF16)
    return hi, mid, lo


def _sel_dot(m01, x):
    return _dot(jnp.concatenate([m01, m01, m01], axis=1), jnp.concatenate(_split3(x), axis=0))


def _silu(v):
    hv = 0.5 * v
    return hv * jnp.tanh(hv) + hv


def _log_sigmoid(v):
    return jnp.minimum(v, 0.0) - jnp.log1p(jnp.exp(-jnp.abs(v)))


def _rms_scale(v):
    return lax.rsqrt(jnp.mean(v * v, axis=-1, keepdims=True) + EPS)


def _interleave(stage_generators):
    live = list(stage_generators)
    while live:
        for g in list(live):
            try:
                next(g)
            except StopIteration:
                live.remove(g)


def _mix_kernel(x_ref, gmix_ref, win_ref,
                hgc_ref, hgng_ref, lc_ref, hgmask_ref, headblk_ref,
                bblk_ref, cblk_ref, s5a_ref, s5d_ref, wglu_ref,
                convw_ref, convb_ref, wqk_ref, wvt_ref, wgx_ref, wgm_ref, bg_ref,
                l128_ref, selpair_ref, mlmask_t_ref, mlng_ref, mlskip_ref, wout_ref,
                o_ref,
                hst_ref, s5st_ref, bu_ref, ct_ref, n_ref, msb_ref, prev_ref, mh_ref):
    si = pl.program_id(1)

    @pl.when(si == 0)
    def _reset_state():
        hst_ref[...] = jnp.zeros_like(hst_ref)
        s5st_ref[...] = jnp.zeros_like(s5st_ref)
        ct_ref[...] = jnp.zeros_like(ct_ref)
        n_ref[...] = jnp.zeros_like(n_ref)
        msb_ref[...] = jnp.zeros_like(msb_ref)
        prev_ref[...] = jnp.zeros_like(prev_ref)

    x = x_ref[...].reshape(R, D_MODEL)
    h = (x * _rms_scale(x) * gmix_ref[...]).astype(BF16)
    res = {}

    def hgrn():
        zall = _dot(h, win_ref[:, _O_HQ:_O_HQ + 4 * HG_W])
        zq, zf, vi, zg = (zall[:, i * HG_W:(i + 1) * HG_W] for i in range(4))
        yield
        q = _silu(zq)
        lb = hgc_ref[0:1, :]
        one_mlb = hgc_ref[1:2, :]
        e = jnp.exp(-jnp.abs(zf))
        rcp = 1.0 / (1.0 + e)
        small = e * rcp
        pos = zf >= 0.0
        logf = jnp.log(lb + one_mlb * jnp.where(pos, rcp, small))
        kk = one_mlb * jnp.where(pos, small, rcp)
        lc = lc_ref[...]
        b = jnp.concatenate([_sel_dot(lc, logf[i * HG_CUM:(i + 1) * HG_CUM]) for i in range(R // HG_CUM)],
                            axis=0)
        yield
        b3 = b.reshape(R // HG_L, HG_L, HG_W)
        blast = jnp.broadcast_to(b3[:, HG_L - 1:HG_L, :], b3.shape).reshape(R, HG_W)
        qd = (q * jnp.exp(b)).astype(BF16)
        ki = kk * jnp.exp(-b)
        ke = (kk * jnp.exp(blast - b)).astype(BF16)
        dec = jnp.exp(blast)
        vib = vi.astype(BF16)
        lane = lax.broadcasted_iota(jnp.int32, (1, HG_W), 1)
        head_lane = [(lane // HG_DK) == hh for hh in range(HG_H)]
        hgmask = hgmask_ref[...] > 0.5
        GR = 128
        scs = []
        for g in range(R // GR):
            rs = slice(g * GR, (g + 1) * GR)
            ki_g = ki[rs]
            kblk = jnp.concatenate([jnp.where(head_lane[hh], ki_g, 0.0) for hh in range(HG_H)], axis=0)
            scs.append(_dot_nt(qd[rs], kblk.astype(BF16)))
        yield
        oi_parts = []
        for g in range(R // GR):
            rs = slice(g * GR, (g + 1) * GR)
            vi_g = vi[rs]
            sc = jnp.where(hgmask, scs[g], 0.0)
            vstk = jnp.concatenate([jnp.where(head_lane[hh], vi_g, 0.0) for hh in range(HG_H)], axis=0)
            oi_parts.append(_dot(sc.astype(BF16), vstk.astype(BF16)))
        o_intra = jnp.concatenate(oi_parts, axis=0)
        yield
        headblk_bf = headblk_ref[...]
        headblk = headblk_bf.astype(F32) > 0.5
        sts = [hst_ref[bi] for bi in range(NB)]
        hoi = [[None] * (TT // HG_L) for _ in range(NB)]
        for cc in range(TT // HG_L):
            dsts = []
            for bi in range(NB):
                r0 = bi * TT + cc * HG_L
                rows = slice(r0, r0 + HG_L)
                hoi[bi][cc] = _dot_nt(qd[rows], sts[bi].astype(BF16))
                dsts.append(_dot_tn(vib[rows], ke[rows]))
            yield
            for bi in range(NB):
                r0 = bi * TT + cc * HG_L
                sts[bi] = dec[r0:r0 + 1, :] * sts[bi] + jnp.where(headblk, dsts[bi], 0.0)
        for bi in range(NB):
            hst_ref[bi] = sts[bi]
        o = o_intra + jnp.concatenate([part for per_b in hoi for part in per_b], axis=0)
        o2 = o * o
        o2_hi = o2.astype(BF16)
        o2_lo = (o2 - o2_hi.astype(F32)).astype(BF16)
        ms_h = (_dot(o2_hi, headblk_bf) + _dot(o2_lo, headblk_bf)) * (1.0 / HG_DK)
        yield
        o_hg = o * lax.rsqrt(ms_h + EPS) * hgng_ref[...] * _silu(zg)
        res['hg'] = _dot(o_hg.astype(BF16), wout_ref[0:HG_W, :])

    def s5():
        u = _dot(h, win_ref[:, _O_SU:_O_SU + S5_W])
        yield
        u_tb = jnp.swapaxes(u.reshape(NB, TT, S5_W), 0, 1).reshape(R, S5_W).astype(BF16)
        yield
        bu_ref[...] = _dot(u_tb, bblk_ref[...]).reshape(TT, NB, 2 * S5_N)
        yield
        ar = s5a_ref[0:1, :]
        ai = s5a_ref[1:2, :]
        x0 = s5st_ref[...]
        xr, xi = x0[:, :S5_N], x0[:, S5_N:]
        for t in range(TT):
            but = bu_ref[t]
            xr, xi = ar * xr - ai * xi + but[:, :S5_N], ar * xi + ai * xr + but[:, S5_N:]
            bu_ref[t] = jnp.concatenate([xr, xi], axis=-1)
            if t % S5_SCAN_STAGE == S5_SCAN_STAGE - 1:
                yield
        s5st_ref[...] = jnp.concatenate([xr, xi], axis=-1)
        xs = bu_ref[...].reshape(R, 2 * S5_N)
        y_tb = _dot(xs.astype(BF16), cblk_ref[...])
        yield
        y = jnp.swapaxes(y_tb.reshape(TT, NB, S5_W), 0, 1).reshape(R, S5_W) + s5d_ref[...] * u
        yield
        y = 0.5 * y * (1.0 + jnp.tanh(math.sqrt(2.0 / math.pi) * (y + 0.044715 * (y * y * y))))
        glu = _dot(y.astype(BF16), wglu_ref[...])
        yield
        o_s5 = y * (0.5 * jnp.tanh(0.5 * glu) + 0.5)
        res['s5'] = _dot(o_s5.astype(BF16), wout_ref[HG_W:HG_W + S5_W, :])

    def mlstm():
        xm = _dot(h, win_ref[:, _O_MX:_O_MX + ML_W])
        yield
        tail = prev_ref[...]
        xm3 = xm.reshape(NB, TT, ML_W)
        prev_ref[...] = xm3[:, TT - 8:, :].reshape(NB * 8, ML_W)
        t8 = lax.broadcasted_iota(jnp.int32, (1, 8, 1), 1)
        conv = xm * convw_ref[CONV_K - 1:CONV_K, :]
        for k in range(1, CONV_K):
            rolled = pltpu.roll(xm, k, 0).reshape(NB, TT, ML_W)
            tail_k = pltpu.roll(tail, NB * 8 - 8 + k, 0).reshape(NB, 8, ML_W)
            head = jnp.where(t8 >= k, rolled[:, 0:8, :], tail_k)
            shifted = jnp.concatenate([head, rolled[:, 8:, :]], axis=1).reshape(R, ML_W)
            conv = conv + shifted * convw_ref[CONV_K - 1 - k:CONV_K - k, :]
        conv = conv + convb_ref[...]
        xc = _silu(conv)
        xcb = xc.astype(BF16)
        xmb = xm.astype(BF16)
        q_hs, k_hs, vt_hs = [], [], []
        for hh in range(ML_H):
            ls = slice(hh * ML_DH, (hh + 1) * ML_DH)
            qk = _dot(xcb[:, ls], wqk_ref[hh])
            q_hs.append(qk[:, :ML_DH])
            k_hs.append(qk[:, ML_DH:] * (ML_DH ** -0.5))
            vt_hs.append(_dot_nt(wvt_ref[hh], xmb[:, ls]))
        yield
        z = _dot(h, win_ref[:, _O_MZ:_O_MZ + ML_W])
        q_all = jnp.concatenate(q_hs, axis=-1).astype(BF16)
        k_all = jnp.concatenate(k_hs, axis=-1).astype(BF16)
        vt_all = jnp.concatenate(vt_hs, axis=0).astype(BF16)
        g = _dot_nt(wgx_ref[...], xcb) + _dot_nt(wgm_ref[...], xmb) + bg_ref[...]
        yield
        pairs = range(NB // 2)
        tls = [slice(p * 128, (p + 1) * 128) for p in pairs]
        l128 = l128_ref[...]
        lsg3 = jnp.concatenate(_split3(_log_sigmoid(g)), axis=0)
        cs = _dot(jnp.concatenate([lsg3[:, tl] for tl in tls], axis=0), l128)
        b16 = jnp.concatenate([cs[48 * p:48 * p + 16] + cs[48 * p + 16:48 * p + 32] + cs[48 * p + 32:48 * p + 48]
                               for p in pairs], axis=-1)
        yield
        ig8 = g[0:8]
        b8 = b16[8:16]
        v8 = ig8 - b8
        tlane = lax.broadcasted_iota(jnp.int32, (1, R), 1) % TT
        cm = v8
        for sh in (1, 2, 4, 8, 16, 32):
            cm = jnp.maximum(cm, jnp.where(tlane >= sh, pltpu.roll(cm, sh, 1), -jnp.inf))
        yield
        selpair = selpair_ref[...]
        cb3 = jnp.concatenate(_split3(jnp.concatenate([cm, b8], axis=0)), axis=0)
        ss = _dot(jnp.concatenate([cb3[:, tl] for tl in tls], axis=0), selpair)
        sb = jnp.concatenate([ss[48 * p:48 * p + 16] + ss[48 * p + 16:48 * p + 32] + ss[48 * p + 32:48 * p + 48]
                              for p in pairs], axis=-1)
        yield
        vmax_sb = sb[0:8]
        glast_sb = sb[8:16]
        m_prev_sb = msb_ref[...]
        m_loc_sb = glast_sb + vmax_sb
        m_new_sb = jnp.maximum(glast_sb + m_prev_sb, m_loc_sb)
        sp_sb = jnp.exp(glast_sb + m_prev_sb - m_new_sb)
        sl_sb = jnp.exp(m_loc_sb - m_new_sb)
        msb_ref[...] = m_new_sb
        lane128 = lax.broadcasted_iota(jnp.int32, (1, 128), 1)
        first_half = lane128 < TT

        def per_time(sbarr):
            parts = [jnp.where(first_half, sbarr[:, (2 * p) * 128:(2 * p + 1) * 128],
                               sbarr[:, (2 * p + 1) * 128:(2 * p + 2) * 128]) for p in range(NB // 2)]
            return jnp.concatenate(parts, axis=-1)

        m_prev_t = per_time(m_prev_sb)
        mbig8 = jnp.maximum(cm, m_prev_t)
        scl8 = jnp.exp(m_prev_t - mbig8)
        en8 = jnp.exp(-(b8 + mbig8))
        w2_8 = per_time(sl_sb) * jnp.exp(v8 - per_time(vmax_sb))

        row16 = lax.broadcasted_iota(jnp.int32, (16, 1), 0)
        v_pieces = [piece.astype(F32) for piece in _split3(v8)]
        m_pieces = [piece.astype(F32) for piece in _split3(mbig8)]
        ones_hi = ((row16 >= 3) & (row16 < 6)).astype(F32)
        ones_lo = (row16 < 3).astype(F32)
        mlmask_t = mlmask_t_ref[...] > 0.5
        second_half = jnp.logical_not(first_half)
        n_rows = ((row16 == 0) & first_half) | ((row16 == 1) & second_half)

        a_ops, b_ops = [], []
        for hh in range(ML_H):
            a_op = ones_hi
            b_op = ones_lo
            for i in range(3):
                a_op = a_op + jnp.where(row16 == i, jnp.broadcast_to(v_pieces[i][hh:hh + 1], (16, R)), 0.0)
                b_op = b_op - jnp.where(row16 == 3 + i, jnp.broadcast_to(m_pieces[i][hh:hh + 1], (16, R)), 0.0)
            a_ops.append(a_op.astype(BF16))
            b_ops.append(b_op.astype(BF16))
        zblk = jnp.zeros((16, 128), BF16)
        logdm_all = []
        for tl in tls:
            lhs = jnp.concatenate([a_op[:, tl] for a_op in a_ops], axis=0)
            rhs = jnp.concatenate(
                [jnp.concatenate([b_ops[hh][:, tl] if j == hh else zblk for j in range(ML_H)], axis=1)
                 for hh in range(ML_H)], axis=0)
            logdm_all.append(_dot_tn(lhs, rhs))

        for hh in range(ML_H):
            ls = slice(hh * ML_DH, (hh + 1) * ML_DH)
            q_ps = [q_all[tl, ls] for tl in tls]
            k_ps = [k_all[tl, ls] for tl in tls]
            vt_ps = [vt_all[ls, tl] for tl in tls]
            logdms = [ld[:, hh * 128:(hh + 1) * 128] for ld in logdm_all]
            cts, n_olds, qprods = [], [], []
            for p in pairs:
                i0 = (2 * p) * ML_H + hh
                ct0 = ct_ref[i0]
                ct1 = ct_ref[i0 + ML_H]
                n_old = n_ref[p * ML_H + hh]
                cts.append((ct0, ct1))
                n_olds.append(n_old)
                lhs = jnp.concatenate([k_ps[p], ct0.astype(BF16), ct1.astype(BF16), n_old.astype(BF16)], axis=0)
                qprods.append(_dot_nt(lhs, q_ps[p]))
            yield
            w_ts, num_ts, inter_ts, qns = [], [], [], []
            for p in pairs:
                w_t = qprods[p][0:128] * jnp.where(mlmask_t, jnp.exp(logdms[p]), 0.0)
                w_ts.append(w_t)
                num_ts.append(_dot(vt_ps[p], w_t.astype(BF16)))
                inter_ts.append(jnp.where(first_half, qprods[p][128:256], qprods[p][256:384]))
                qns.append(qprods[p][384:400])
            yield
            for p in pairs:
                tl = tls[p]
                den_r = jnp.sum(w_ts[p], axis=0, keepdims=True)
                qn_r = jnp.where(first_half, qns[p][0:1], qns[p][1:2])
                scl_r = scl8[hh:hh + 1, tl]
                den = den_r + scl_r * qn_r
                inv_r = 1.0 / jnp.maximum(jnp.abs(den), en8[hh:hh + 1, tl])
                h_t = (num_ts[p] + scl_r * inter_ts[p]) * inv_r
                hn_t = h_t * lax.rsqrt(jnp.mean(h_t * h_t, axis=0, keepdims=True) + EPS)
                mh_ref[tl, ls] = hn_t.T
                w2_r = w2_8[hh:hh + 1, tl]
                vtw = vt_ps[p].astype(F32) * w2_r
                lhs = jnp.concatenate([jnp.where(first_half, vtw, 0.0).astype(BF16),
                                       jnp.where(second_half, vtw, 0.0).astype(BF16),
                                       jnp.where(n_rows, jnp.broadcast_to(w2_r, (16, 128)), 0.0).astype(BF16)],
                                      axis=0)
                kprod = _dot(lhs, k_ps[p])
                dct0, dct1, dn = kprod[0:128], kprod[128:256], kprod[256:272]
                sp0 = sp_sb[hh:hh + 1, (2 * p) * 128:(2 * p + 1) * 128]
                sp1 = sp_sb[hh:hh + 1, (2 * p + 1) * 128:(2 * p + 2) * 128]
                i0 = (2 * p) * ML_H + hh
                ct_ref[i0] = sp0 * cts[p][0] + dct0
                ct_ref[i0 + ML_H] = sp1 * cts[p][1] + dct1
                n_ref[p * ML_H + hh] = jnp.where(row16 == 0, sp0, sp1) * n_olds[p] + dn
            yield
        o_ml = (mh_ref[...] * mlng_ref[...] + mlskip_ref[...] * xc) * _silu(z)
        res['ml'] = _dot(o_ml.astype(BF16), wout_ref[HG_W + S5_W:, :])

    _interleave([s5(), mlstm(), hgrn()])

    out = x + (res['hg'] + res['s5'] + res['ml'])
    o_ref[...] = out.reshape(NB, TT, D_MODEL)


def _ffn_kernel(x_ref, g_ref, wg_ref, wu_ref, wd_ref, gfin_ref, o_ref, *, final_norm):
    x = x_ref[...]
    h = (x * _rms_scale(x) * g_ref[...]).astype(BF16)
    acc = x
    for c in range(D_FF // FFN_TF):
        cs = slice(c * FFN_TF, (c + 1) * FFN_TF)
        gate = _dot(h, wg_ref[:, cs])
        up = _dot(h, wu_ref[:, cs])
        act = (_silu(gate) * up).astype(BF16)
        acc = acc + _dot(act, wd_ref[cs, :])
    if final_norm:
        acc = acc * _rms_scale(acc) * gfin_ref[...]
    o_ref[...] = acc


def _np_masks():
    icum = np.arange(HG_CUM)
    lc = ((icum[:, None] // HG_L) == (icum[None, :] // HG_L)) & (icum[None, :] <= icum[:, None])
    i256 = np.arange(HG_W)
    i128 = np.arange(128)
    hg_causal = ((i128[:, None] // HG_L) == (i128[None, :] // HG_L)) & (i128[None, :] <= i128[:, None])
    hgmask = np.tile(hg_causal, (1, HG_H))
    headblk = (i256[:, None] // HG_DK) == (i256[None, :] // HG_DK)
    mlmask_t = ((i128[:, None] // TT) == (i128[None, :] // TT)) & (i128[:, None] <= i128[None, :])
    selpair = np.zeros((128, 256), np.float32)
    for j in range(2):
        selpair[j * TT + TT - 1, j * 128:(j + 1) * 128] = 1.0
    return dict(lc=lc.astype(np.float32), hgmask=hgmask.astype(np.float32),
                headblk=headblk.astype(np.float32), mlmask_t=mlmask_t.astype(np.float32), selpair=selpair)


_MASKS = _np_masks()


def _const_spec(shape):
    nd = len(shape)
    return pl.BlockSpec(shape, lambda *_: (0,) * nd, pipeline_mode=pl.Buffered(1))


class _Layer:
    def __init__(self, stacked, layer):
        self.stacked, self.layer = stacked, layer


def _operand(c):
    if isinstance(c, _Layer):
        layer = c.layer
        zeros = (0,) * (c.stacked.ndim - 1)
        spec = pl.BlockSpec((None,) + c.stacked.shape[1:], lambda *_: (layer,) + zeros,
                            pipeline_mode=pl.Buffered(1))
        return c.stacked, spec
    return c, _const_spec(c.shape)


def _pad_rows(a, rows):
    return jnp.concatenate([a, jnp.zeros((rows - a.shape[0],) + a.shape[1:], a.dtype)], axis=0)


def _s5_params(lam_re, lam_im, log_dt, b_re, b_im, c_re, c_im):
    lre = lam_re.astype(F32)
    lim = lam_im.astype(F32)
    dt = jnp.exp(log_dt.astype(F32))[:, None]
    mag = jnp.exp(lre * dt)
    ang = lim * dt
    ar, ai = mag * jnp.cos(ang), mag * jnp.sin(ang)
    den = lre * lre + lim * lim
    nr, ni = ar - 1.0, ai
    cr = (nr * lre + ni * lim) / den
    ci = (ni * lre - nr * lim) / den
    bbr = cr[..., None] * b_re - ci[..., None] * b_im
    bbi = cr[..., None] * b_im + ci[..., None] * b_re
    eye = jnp.eye(S5_G, dtype=F32)
    blk_r = jnp.einsum('gpc,gk->gckp', bbr, eye).reshape(S5_W, S5_N)
    blk_i = jnp.einsum('gpc,gk->gckp', bbi, eye).reshape(S5_W, S5_N)
    bblk = jnp.concatenate([blk_r, blk_i], axis=1).astype(BF16)
    cb_r = jnp.einsum('gcp,gk->kpgc', c_re, eye).reshape(S5_N, S5_W)
    cb_i = jnp.einsum('gcp,gk->kpgc', c_im, eye).reshape(S5_N, S5_W)
    cblk = jnp.concatenate([cb_r, -cb_i], axis=0).astype(BF16)
    a = _pad_rows(jnp.stack([ar.reshape(S5_N), ai.reshape(S5_N)], axis=0), 8)
    return bblk, cblk, a


def _layer_params(gmix, lb, hg_ng, lam_re, lam_im, log_dt, b_re, b_im, c_re, c_im, s5_d, w_glu, conv_w, conv_b,
                  wq, wk, wv, w_gates, b_gates, ml_ng, ml_skip):
    hgc = _pad_rows(jnp.stack([lb, 1.0 - lb], axis=0), 8)
    bblk, cblk, s5a = _s5_params(lam_re, lam_im, log_dt, b_re, b_im, c_re, c_im)
    wg3 = w_gates.reshape(3, ML_H, ML_DH, 2 * ML_H)
    wgx = (jnp.einsum('hde,hec->hdc', wq, wg3[0]) + (ML_DH ** -0.5) * jnp.einsum('hde,hec->hdc', wk, wg3[1]))
    wgm = jnp.einsum('hde,hec->hdc', wv, wg3[2])
    zrow = jnp.zeros((8 - ML_H, ML_W), F32)

    def gate_rows(w):
        wt = w.reshape(ML_W, 2 * ML_H).T
        return jnp.concatenate([wt[:ML_H], zrow, wt[ML_H:], zrow], axis=0).astype(BF16)

    zb = jnp.zeros((8 - ML_H,), F32)
    bg = jnp.broadcast_to(jnp.concatenate([b_gates[:ML_H], zb, b_gates[ML_H:], zb])[:, None], (16, R))
    return dict(
        gmix=gmix[None, :], hgc=hgc, hg_ng=hg_ng[None, :], bblk=bblk, cblk=cblk, s5a=s5a, s5_d=s5_d[None, :],
        w_glu=w_glu.astype(BF16), conv_w=_pad_rows(conv_w, 8), conv_b=conv_b[None, :],
        wqk=jnp.concatenate([wq, wk], axis=-1).astype(BF16), wvt=jnp.swapaxes(wv, 1, 2).astype(BF16),
        wgx=gate_rows(wgx), wgm=gate_rows(wgm), bg=bg, ml_ng=ml_ng[None, :], ml_skip=ml_skip[None, :])


def _mix_layer(x, layer, lp, w_in, w_out):
    B, S, _ = x.shape
    assert B % NB == 0 and S % TT == 0
    mk = _MASKS
    p = {k: _Layer(v, layer) for k, v in lp.items()}
    consts = [
        p['gmix'], _Layer(w_in, layer),
        p['hgc'], p['hg_ng'], jnp.asarray(mk['lc'], BF16), jnp.asarray(mk['hgmask'], F32),
        jnp.asarray(mk['headblk'], BF16),
        p['bblk'], p['cblk'], p['s5a'], p['s5_d'],
        p['w_glu'],
        p['conv_w'], p['conv_b'], p['wqk'], p['wvt'],
        p['wgx'], p['wgm'], p['bg'],
        jnp.asarray(mk['mlmask_t'], BF16), jnp.asarray(mk['selpair'], BF16), jnp.asarray(mk['mlmask_t'], F32),
        p['ml_ng'], p['ml_skip'], _Layer(w_out, layer),
    ]
    operands = [_operand(c) for c in consts]
    xspec = pl.BlockSpec((NB, TT, D_MODEL), lambda bi, si: (bi, si, 0))
    scratch = [
        pltpu.VMEM((NB, HG_W, HG_W), F32),
        pltpu.VMEM((NB, 2 * S5_N), F32),
        pltpu.VMEM((TT, NB, 2 * S5_N), F32),
        pltpu.VMEM((NB * ML_H, ML_DH, ML_DH), F32),
        pltpu.VMEM((NB // 2 * ML_H, 16, ML_DH), F32),
        pltpu.VMEM((8, NB * 128), F32),
        pltpu.VMEM((NB * 8, ML_W), F32),
        pltpu.VMEM((R, ML_W), F32),
    ]
    return pl.pallas_call(
        _mix_kernel,
        grid=(B // NB, S // TT),
        in_specs=[xspec] + [spec for _, spec in operands],
        out_specs=xspec,
        out_shape=jax.ShapeDtypeStruct(x.shape, F32),
        scratch_shapes=scratch,
        compiler_params=pltpu.CompilerParams(
            dimension_semantics=("parallel", "arbitrary"), vmem_limit_bytes=VMEM_LIMIT),
        name="mixer",
    )(x, *[arr for arr, _ in operands])


def _ffn_layer(x2, g, wg, wu, wd, gfin, final_norm):
    T = x2.shape[0]
    assert T % FFN_TM == 0
    operands = [_operand(c) for c in (g, wg, wu, wd, gfin[None, :])]
    xspec = pl.BlockSpec((FFN_TM, D_MODEL), lambda i: (i, 0))
    return pl.pallas_call(
        functools.partial(_ffn_kernel, final_norm=final_norm),
        grid=(T // FFN_TM,),
        in_specs=[xspec] + [spec for _, spec in operands],
        out_specs=xspec,
        out_shape=jax.ShapeDtypeStruct(x2.shape, F32),
        compiler_params=pltpu.CompilerParams(
            dimension_semantics=("parallel",), vmem_limit_bytes=VMEM_LIMIT),
        name="ffn",
    )(x2, *[arr for arr, _ in operands])


def kernel(x, norm_mix_g, w_in, hgrn_lb_raw, hgrn_norm_g, s5_lambda_re, s5_lambda_im, s5_log_dt, s5_b_re, s5_b_im, s5_c_re, s5_c_im, s5_d, s5_w_glu, ml_conv_w, ml_conv_b, ml_wq, ml_wk, ml_wv, ml_w_gates, ml_b_gates, ml_norm_g, ml_skip, w_out, norm_ffn_g, ffn_w_gate, ffn_w_up, ffn_w_down, norm_final_g):
    B, S, D = x.shape
    depth = w_in.shape[0]
    lb_cs = jnp.cumsum(jax.nn.softmax(hgrn_lb_raw.astype(F32), axis=0), axis=0)
    lb_all = lb_cs - lb_cs[0]
    w_in_b, w_out_b = w_in.astype(BF16), w_out.astype(BF16)
    wg_b, wu_b, wd_b = ffn_w_gate.astype(BF16), ffn_w_up.astype(BF16), ffn_w_down.astype(BF16)
    lp = jax.vmap(_layer_params)(
        norm_mix_g, lb_all, hgrn_norm_g, s5_lambda_re, s5_lambda_im, s5_log_dt, s5_b_re, s5_b_im, s5_c_re,
        s5_c_im, s5_d, s5_w_glu, ml_conv_w, ml_conv_b, ml_wq, ml_wk, ml_wv, ml_w_gates, ml_b_gates,
        ml_norm_g, ml_skip)
    g_ffn = norm_ffn_g[:, None, :]
    for l in range(depth):
        x = _mix_layer(x, l, lp, w_in_b, w_out_b)
        x2 = _ffn_layer(x.reshape(B * S, D), _Layer(g_ffn, l), _Layer(wg_b, l), _Layer(wu_b, l), _Layer(wd_b, l),
                        norm_final_g, final_norm=(l == depth - 1))
        x = x2.reshape(B, S, D)
    return x
```
